```python
import math
import jax, jax.numpy as jnp
from jax import lax
import numpy as np

D_MODEL = 1024
BATCH = 4
SEQ = 8192
DEPTH = 1

NORM_EPS = 1e-6
NEG_INF = -1e30

CHUNK = 128
GMLP_WIDTH = D_MODEL
GMLP_GROUPS = 8
GMLP_GROUP_DIM = GMLP_WIDTH // GMLP_GROUPS

ATTN_PATTERNS = ((128, 1), (512, 4), (2048, 16))
N_ATTN_GROUPS = len(ATTN_PATTERNS)
ATTN_HEADS = 8
HEAD_DIM = 64
ATTN_GROUP_WIDTH = ATTN_HEADS * HEAD_DIM
QKV_WIDTH = N_ATTN_GROUPS * ATTN_GROUP_WIDTH
ATTN_OUT_WIDTH = ATTN_GROUP_WIDTH
Q_BLOCK = 128

IN_WIDTH = 2 * GMLP_WIDTH + 3 * QKV_WIDTH + 2 * D_MODEL

D_FF = int(math.ceil(8 * D_MODEL / 3 / 256)) * 256

kernel_name = 'hybrid_gmlp_dilated_attn_block'


def rms_norm(x, g):
    x32 = x.astype(jnp.float32)
    y = x32 * lax.rsqrt(jnp.mean(x32 * x32, axis=-1, keepdims=True) + NORM_EPS)
    return (y * g.astype(jnp.float32)).astype(x.dtype)


def alibi_slopes():
    n = N_ATTN_GROUPS * ATTN_HEADS
    i = jnp.arange(1, n + 1, dtype=jnp.float32)
    return jnp.exp2(-8.0 * i / n).reshape(N_ATTN_GROUPS, ATTN_HEADS)


def spatial_gating(u, v, ln_g, ln_b, ws, bs):
    b, s, _ = u.shape
    u32 = jax.nn.gelu(u.astype(jnp.float32), approximate=False)
    v32 = jax.nn.gelu(v.astype(jnp.float32), approximate=False)
    mean = jnp.mean(v32, axis=-1, keepdims=True)
    var = jnp.mean(jnp.square(v32 - mean), axis=-1, keepdims=True)
    vn = (v32 - mean) * lax.rsqrt(var + NORM_EPS) * ln_g.astype(jnp.float32) + ln_b.astype(jnp.float32)
    vc = vn.reshape(b, s // CHUNK, CHUNK, GMLP_GROUPS, GMLP_GROUP_DIM)
    mixed = jnp.einsum('gts,bnsgc->bntgc', ws.astype(jnp.float32), vc)
    mixed = mixed + bs.astype(jnp.float32).T[:, :, None]
    return (u32 * mixed.reshape(b, s, GMLP_WIDTH)).astype(u.dtype)


def dilated_window_attention(q, k, v, window, dilation, slopes):
    b, s, h, dh = q.shape
    half = window // (2 * dilation)
    length = s // dilation
    n_blk = -(-length // Q_BLOCK)
    padded = n_blk * Q_BLOCK

    def to_sub(t):
        t = t.astype(jnp.float32).reshape(b, length, dilation, h, dh).transpose(0, 2, 1, 3, 4)
        return jnp.pad(t, ((0, 0), (0, 0), (0, padded - length), (0, 0), (0, 0)))

    def neighbours(t):
        tp = jnp.pad(t, ((0, 0), (0, 0), (Q_BLOCK, Q_BLOCK), (0, 0), (0, 0)))
        views = [tp[:, :, o * Q_BLOCK:o * Q_BLOCK + padded].reshape(b, dilation, n_blk, Q_BLOCK, h, dh)
                 for o in range(3)]
        return jnp.concatenate(views, axis=3)

    qb = to_sub(q).reshape(b, dilation, n_blk, Q_BLOCK, h, dh)
    kb = neighbours(to_sub(k))
    vb = neighbours(to_sub(v))

    qi = jnp.arange(Q_BLOCK)[:, None]
    kj = jnp.arange(3 * Q_BLOCK)[None, :]
    rel = kj - Q_BLOCK - qi
    key_idx = jnp.arange(n_blk)[:, None, None] * Q_BLOCK + (kj - Q_BLOCK)[None]
    valid = (jnp.abs(rel) <= half)[None] & (key_idx >= 0) & (key_idx < length)
    dist = (jnp.abs(rel) * dilation).astype(jnp.float32)

    scores = jnp.einsum('brnqhe,brnkhe->brnhqk', qb, kb) * (1.0 / math.sqrt(dh))
    scores = scores - slopes[:, None, None] * dist
    scores = jnp.where(valid[:, None], scores, NEG_INF)
    lse = jax.nn.logsumexp(scores, axis=-1)
    probs = jnp.exp(scores - lse[..., None])
    out = jnp.einsum('brnhqk,brnkhe->brnqhe', probs, vb)

    out = out.reshape(b, dilation, padded, h, dh)[:, :, :length]
    out = out.transpose(0, 2, 1, 3, 4).reshape(b, s, h, dh)
    lse = lse.transpose(0, 1, 2, 4, 3).reshape(b, dilation, padded, h)[:, :, :length]
    lse = lse.transpose(0, 2, 1, 3).reshape(b, s, h)
    return out, lse


def dilated_attention_mixer(q, k, v):
    b, s, _ = q.shape
    shp = (b, s, N_ATTN_GROUPS, ATTN_HEADS, HEAD_DIM)
    q, k, v = q.reshape(shp), k.reshape(shp), v.reshape(shp)
    slopes = alibi_slopes()
    outs, lses = [], []
    for g, (window, dil) in enumerate(ATTN_PATTERNS):
        o, l = dilated_window_attention(q[:, :, g], k[:, :, g], v[:, :, g], window, dil, slopes[g])
        outs.append(o)
        lses.append(l)
    outs = jnp.stack(outs, axis=0)
    weights = jax.nn.softmax(jnp.stack(lses, axis=0), axis=0)
    y = jnp.sum(weights[..., None] * outs, axis=0)
    return y.reshape(b, s, ATTN_OUT_WIDTH)


def swiglu(x, w_gate, w_up, w_down):
    return (jax.nn.silu(x @ w_gate) * (x @ w_up)) @ w_down


def setup_inputs(seed: int = 0) -> dict:
    key = jax.random.key(seed)
    ks = jax.random.split(key, 16)
    f32 = jnp.float32
    nrm = lambda k, shape, scale: jax.random.normal(k, shape, f32) * scale
    return {
        'x': jax.random.normal(ks[0], (BATCH, SEQ, D_MODEL), f32),
        'norm_mix_g': 1.0 + nrm(ks[1], (DEPTH, D_MODEL), 0.02),
        'w_in': nrm(ks[2], (DEPTH, D_MODEL, IN_WIDTH), D_MODEL ** -0.5),
        'gmlp_ln_g': 1.0 + nrm(ks[3], (DEPTH, GMLP_WIDTH), 0.02),
        'gmlp_ln_b': nrm(ks[4], (DEPTH, GMLP_WIDTH), 0.02),
        'gmlp_ws': nrm(ks[5], (DEPTH, GMLP_GROUPS, CHUNK, CHUNK), 0.5 * CHUNK ** -0.5),
        'gmlp_bs': 1.0 + nrm(ks[6], (DEPTH, GMLP_GROUPS, CHUNK), 0.1),
        'w_branch_gmlp': nrm(ks[7], (DEPTH, GMLP_WIDTH, D_MODEL), GMLP_WIDTH ** -0.5),
        'w_branch_attn': nrm(ks[8], (DEPTH, ATTN_OUT_WIDTH, D_MODEL), ATTN_OUT_WIDTH ** -0.5),
        'w_out': nrm(ks[9], (DEPTH, D_MODEL, D_MODEL), D_MODEL ** -0.5),
        'norm_ffn_g': 1.0 + nrm(ks[10], (DEPTH, D_MODEL), 0.02),
        'w_ffn_gate': nrm(ks[11], (DEPTH, D_MODEL, D_FF), D_MODEL ** -0.5),
        'w_ffn_up': nrm(ks[12], (DEPTH, D_MODEL, D_FF), D_MODEL ** -0.5),
        'w_ffn_down': nrm(ks[13], (DEPTH, D_FF, D_MODEL), D_FF ** -0.5),
        'norm_final_g': 1.0 + nrm(ks[14], (D_MODEL,), 0.02),
    }


def reference(x, norm_mix_g, w_in, gmlp_ln_g, gmlp_ln_b, gmlp_ws, gmlp_bs, w_branch_gmlp,
              w_branch_attn, w_out, norm_ffn_g, w_ffn_gate, w_ffn_up, w_ffn_down, norm_final_g):
    widths = [GMLP_WIDTH, GMLP_WIDTH, QKV_WIDTH, QKV_WIDTH, QKV_WIDTH, D_MODEL, D_MODEL]
    splits = [int(c) for c in np.cumsum(widths)[:-1]]
    h = x
    for l in range(DEPTH):
        xn = rms_norm(h, norm_mix_g[l])
        proj = xn @ w_in[l]
        u, vg, q, k, va, gate_a, gate_b = jnp.split(proj, splits, axis=-1)
        y_gmlp = spatial_gating(u, vg, gmlp_ln_g[l], gmlp_ln_b[l], gmlp_ws[l], gmlp_bs[l])
        y_attn = dilated_attention_mixer(q, k, va).astype(h.dtype)
        merged = (jax.nn.sigmoid(gate_a) * (y_gmlp @ w_branch_gmlp[l])
                  + jax.nn.sigmoid(gate_b) * (y_attn @ w_branch_attn[l]))
        h = h + merged @ w_out[l]
        h = h + swiglu(rms_norm(h, norm_ffn_g[l]), w_ffn_gate[l], w_ffn_up[l], w_ffn_down[l])
    return rms_norm(h, norm_final_g)
```

```python
import functools
import math

import jax
import jax.numpy as jnp
from jax import lax
from jax.experimental import pallas as pl
from jax.experimental.pallas import tpu as pltpu

F32 = jnp.float32
BF16 = jnp.bfloat16

D_MODEL = 1024
NORM_EPS = 1e-6
NEG_INF = -1e30

CHUNK = 128
GMLP_WIDTH = D_MODEL
GMLP_GROUPS = 8

ATTN_PATTERNS = ((128, 1), (512, 4), (2048, 16))
N_GROUPS = len(ATTN_PATTERNS)
HEADS = 8
HEAD_DIM = 64
GROUP_WIDTH = HEADS * HEAD_DIM
QKV_WIDTH = N_GROUPS * GROUP_WIDTH
Q_BLOCK = 128
HALF = 64
KEY_WIN = Q_BLOCK + 2 * HALF
HEAD_PAIR = 2 * HEAD_DIM

OFF_U = 0
OFF_V = OFF_U + GMLP_WIDTH
OFF_Q = OFF_V + GMLP_WIDTH
OFF_K = OFF_Q + QKV_WIDTH
OFF_VA = OFF_K + QKV_WIDTH
OFF_GA = OFF_VA + QKV_WIDTH
OFF_GB = OFF_GA + D_MODEL
IN_WIDTH = OFF_GB + D_MODEL

V7X_VMEM_BYTES = 64 * 1024 * 1024
MIX_IN_ROWS = 256
MIX_OUT_ROWS = 256
ATTN_ROWS = 512


def _vmem_limit(nbytes):
    return int(min(nbytes * 3 // 2, V7X_VMEM_BYTES - (4 << 20)))


def _const_spec(shape):
    zeros = (0,) * len(shape)
    return pl.BlockSpec(shape, lambda *_: zeros, pipeline_mode=pl.Buffered(1))


def _rms_norm(x, g):
    return x * lax.rsqrt(jnp.mean(x * x, axis=-1, keepdims=True) + NORM_EPS) * g


def _dot(a, b):
    return jnp.dot(a, b, preferred_element_type=F32)


def _gelu(x):
    return 0.5 * x * (1.0 + lax.erf(x * math.sqrt(0.5)))


def _mix_in_kernel(x_ref, g_ref, win_ref, lng_ref, lnb_ref, ws_ref, bst_ref, wa_ref,
                   q0_ref, q1_ref, q2_ref, k0_ref, k1_ref, k2_ref, v0_ref, v1_ref, v2_ref,
                   ma_ref, sgb_ref, y_scr):
    rows = x_ref.shape[0]
    xn = _rms_norm(x_ref[...], g_ref[...]).astype(BF16)

    def proj(off, width):
        return _dot(xn, win_ref[:, off:off + width])

    scale = 1.0 / math.sqrt(HEAD_DIM)
    for gi, (q_ref, k_ref, v_ref) in enumerate(
            ((q0_ref, k0_ref, v0_ref), (q1_ref, k1_ref, v1_ref), (q2_ref, k2_ref, v2_ref))):
        off = gi * GROUP_WIDTH
        q_ref[...] = (proj(OFF_Q + off, GROUP_WIDTH) * scale).astype(BF16)
        k_ref[...] = proj(OFF_K + off, GROUP_WIDTH).astype(BF16)
        v_ref[...] = proj(OFF_VA + off, GROUP_WIDTH).astype(BF16)

    sgb_ref[...] = jax.nn.sigmoid(proj(OFF_GB, D_MODEL))

    gu = _gelu(proj(OFF_U, GMLP_WIDTH))
    gv = _gelu(proj(OFF_V, GMLP_WIDTH))
    mean = jnp.mean(gv, axis=-1, keepdims=True)
    cen = gv - mean
    var = jnp.mean(cen * cen, axis=-1, keepdims=True)
    vn = (cen * lax.rsqrt(var + NORM_EPS) * lng_ref[...] + lnb_ref[...]).astype(BF16)

    for c in range(rows // CHUNK):
        rs = slice(c * CHUNK, (c + 1) * CHUNK)
        for g in range(GMLP_GROUPS):
            cs = slice(g * CHUNK, (g + 1) * CHUNK)
            mixed = _dot(ws_ref[g], vn[rs, cs]) + bst_ref[:, g:g + 1]
            y_scr[rs, cs] = (gu[rs, cs] * mixed).astype(BF16)

    ya = _dot(y_scr[...], wa_ref[...])
    ma_ref[...] = jax.nn.sigmoid(proj(OFF_GA, D_MODEL)) * ya


def _mix_in(x2d, norm_g, w_in, ln_g, ln_b, ws, bs_t, w_a):
    tokens = x2d.shape[0]
    rows = MIX_IN_ROWS
    assert tokens % rows == 0 and rows % CHUNK == 0
    row_spec = lambda width: pl.BlockSpec((rows, width), lambda i: (i, 0))
    qkv_shape = jax.ShapeDtypeStruct((tokens, GROUP_WIDTH), BF16)
    f32_shape = jax.ShapeDtypeStruct((tokens, D_MODEL), F32)
    weight_bytes = 2 * (w_in.size + w_a.size + ws.size)
    tile_bytes = 2 * rows * (4 * D_MODEL + 9 * 2 * GROUP_WIDTH + 2 * 4 * D_MODEL)
    temp_bytes = rows * D_MODEL * (2 + 6 * 4)
    return pl.pallas_call(
        _mix_in_kernel,
        grid=(tokens // rows,),
        in_specs=[
            row_spec(D_MODEL),
            _const_spec((1, D_MODEL)),
            _const_spec((D_MODEL, IN_WIDTH)),
            _const_spec((1, GMLP_WIDTH)),
            _const_spec((1, GMLP_WIDTH)),
            _const_spec((GMLP_GROUPS, CHUNK, CHUNK)),
            _const_spec((CHUNK, GMLP_GROUPS)),
            _const_spec((GMLP_WIDTH, D_MODEL)),
        ],
        out_specs=[row_spec(GROUP_WIDTH)] * 9 + [row_spec(D_MODEL)] * 2,
        out_shape=[qkv_shape] * 9 + [f32_shape] * 2,
        scratch_shapes=[pltpu.VMEM((rows, GMLP_WIDTH), BF16)],
        compiler_params=pltpu.CompilerParams(
            dimension_semantics=("parallel",),
            vmem_limit_bytes=_vmem_limit(weight_bytes + tile_bytes + temp_bytes)),
        name="mix_in",
    )(x2d, norm_g, w_in, ln_g, ln_b, ws, bs_t, w_a)


def _attn_kernel(q_ref, k_ref, kl_ref, kr_ref, v_ref, vl_ref, vr_ref, bias_ref,
                 o_ref, lse_ref):
    rows = q_ref.shape[0]
    n_blk = rows // Q_BLOCK
    first = pl.program_id(2) == 0
    last = pl.program_id(2) == pl.num_programs(2) - 1
    lane = lax.broadcasted_iota(jnp.int32, (Q_BLOCK, HEAD_PAIR), 1)
    low_half = lane < HEAD_DIM

    for j in range(n_blk):
        lo = j * Q_BLOCK - HALF
        if j == 0:
            edge = jnp.where(first, 1, 0)
        elif j == n_blk - 1:
            edge = jnp.where(last, 2, 0)
        else:
            edge = 0
        for p in range(HEADS // 2):
            cs = slice(p * HEAD_PAIR, (p + 1) * HEAD_PAIR)

            def window(main_ref, left_ref, right_ref):
                if j == 0:
                    return jnp.concatenate([left_ref[:, cs], main_ref[0:lo + KEY_WIN, cs]], axis=0)
                if j == n_blk - 1:
                    return jnp.concatenate([main_ref[lo:rows, cs], right_ref[:, cs]], axis=0)
                return main_ref[lo:lo + KEY_WIN, cs]

            k_win = window(k_ref, kl_ref, kr_ref)
            v_win = window(v_ref, vl_ref, vr_ref)
            q_pair = q_ref[j * Q_BLOCK:(j + 1) * Q_BLOCK, cs]
            outs, lses = [], []
            for e in range(2):
                own_lanes = low_half if e == 0 else jnp.logical_not(low_half)
                q_head = jnp.where(own_lanes, q_pair, jnp.zeros_like(q_pair))
                s = lax.dot_general(q_head, k_win, (((1,), (1,)), ((), ())),
                                    preferred_element_type=F32)
                s = s + bias_ref[edge, 2 * p + e]
                m = jnp.max(s, axis=-1, keepdims=True)
                prob = jnp.exp(s - m)
                denom = jnp.sum(prob, axis=-1, keepdims=True)
                outs.append(_dot(prob.astype(BF16), v_win) / denom)
                lses.append(m + jnp.log(denom))
            rs = slice(j * Q_BLOCK, (j + 1) * Q_BLOCK)
            o_ref[rs, cs] = jnp.where(low_half, outs[0], outs[1]).astype(o_ref.dtype)
            lse_ref[rs, cs] = jnp.where(low_half, lses[0], lses[1])


def _attn_bias(group, dilation):
    n = N_GROUPS * HEADS
    idx = jnp.arange(1, n + 1, dtype=F32)
    slopes = jnp.exp2(-8.0 * idx / n).reshape(N_GROUPS, HEADS)[group]
    qi = jnp.arange(Q_BLOCK)[:, None]
    kj = jnp.arange(KEY_WIN)[None, :]
    rel = jnp.abs(kj - HALF - qi)
    dist = (rel * dilation).astype(F32)
    band = rel <= HALF
    in_seq = jnp.stack([jnp.ones_like(kj, dtype=bool), kj >= HALF, kj < KEY_WIN - HALF])
    valid = band[None] & in_seq
    score_bias = -slopes[:, None, None] * dist[None]
    return jnp.where(valid[:, None], score_bias[None], NEG_INF)


def _attention_group(group, dilation, q, k, v, batch, seq):
    length = seq // dilation
    rows = min(ATTN_ROWS, length)
    assert length % rows == 0 and rows % Q_BLOCK == 0 and rows >= 2 * Q_BLOCK
    n_tiles = length // rows
    halo_per_tile = rows // HALF
    n_halo = length // HALF
    width = dilation * GROUP_WIDTH
    view = lambda t: t.reshape(batch, length, width)

    main = pl.BlockSpec((None, rows, GROUP_WIDTH), lambda b, r, t: (b, t, r))
    left = pl.BlockSpec((None, HALF, GROUP_WIDTH),
                        lambda b, r, t: (b, jnp.maximum(t * halo_per_tile - 1, 0), r))
    right = pl.BlockSpec((None, HALF, GROUP_WIDTH),
                         lambda b, r, t: (b, jnp.minimum((t + 1) * halo_per_tile, n_halo - 1), r))
    bias = _attn_bias(group, dilation)
    tile_bytes = 2 * (3 * rows * GROUP_WIDTH * 2 + 4 * HALF * GROUP_WIDTH * 2
                      + rows * GROUP_WIDTH * (2 + 4))
    out, lse = pl.pallas_call(
        _attn_kernel,
        grid=(batch, dilation, n_tiles),
        in_specs=[main, main, left, right, main, left, right,
                  _const_spec(bias.shape)],
        out_specs=[main, main],
        out_shape=[jax.ShapeDtypeStruct((batch, length, width), BF16),
                   jax.ShapeDtypeStruct((batch, length, width), F32)],
        compiler_params=pltpu.CompilerParams(
            dimension_semantics=("parallel", "parallel", "parallel"),
            vmem_limit_bytes=_vmem_limit(tile_bytes + bias.size * 4 + (8 << 20))),
        name=f"attn_dil{dilation}",
    )(view(q), view(k), view(k), view(k), view(v), view(v), view(v), bias)
    tokens = batch * seq
    return out.reshape(tokens, GROUP_WIDTH), lse.reshape(tokens, GROUP_WIDTH)


def _mix_out_kernel(x_ref, ma_ref, sgb_ref, o0_ref, o1_ref, o2_ref, l0_ref, l1_ref, l2_ref,
                    wb_ref, wo_ref, gf_ref, wg_ref, wu_ref, wd_ref, gl_ref, out_ref):
    l0, l1, l2 = l0_ref[...], l1_ref[...], l2_ref[...]
    top = jnp.maximum(jnp.maximum(l0, l1), l2)
    e0, e1, e2 = jnp.exp(l0 - top), jnp.exp(l1 - top), jnp.exp(l2 - top)
    y = (e0 * o0_ref[...].astype(F32) + e1 * o1_ref[...].astype(F32)
         + e2 * o2_ref[...].astype(F32)) / (e0 + e1 + e2)
    merged = ma_ref[...] + sgb_ref[...] * _dot(y.astype(BF16), wb_ref[...])
    h = x_ref[...] + _dot(merged.astype(BF16), wo_ref[...])
    hn = _rms_norm(h, gf_ref[...]).astype(BF16)
    act = jax.nn.silu(_dot(hn, wg_ref[...])) * _dot(hn, wu_ref[...])
    h = h + _dot(act.astype(BF16), wd_ref[...])
    out_ref[...] = _rms_norm(h, gl_ref[...])


def _mix_out(x2d, ma, sgb, outs, lses, w_b, w_out, g_ffn, w_gate, w_up, w_down, g_final):
    tokens = x2d.shape[0]
    rows = MIX_OUT_ROWS
    assert tokens % rows == 0
    d_ff = w_gate.shape[1]
    row_spec = lambda width: pl.BlockSpec((rows, width), lambda i: (i, 0))
    weights = (w_b, w_out, w_gate, w_up, w_down)
    weight_bytes = 2 * sum(w.size for w in weights)
    tile_bytes = 2 * rows * (4 * 4 * D_MODEL + 3 * GROUP_WIDTH * (2 + 4))
    temp_bytes = rows * (3 * 4 * d_ff + 4 * 4 * D_MODEL)
    return pl.pallas_call(
        _mix_out_kernel,
        grid=(tokens // rows,),
        in_specs=[row_spec(D_MODEL)] * 3 + [row_spec(GROUP_WIDTH)] * 6 + [
            _const_spec(w_b.shape), _const_spec(w_out.shape), _const_spec((1, D_MODEL)),
            _const_spec(w_gate.shape), _const_spec(w_up.shape), _const_spec(w_down.shape),
            _const_spec((1, D_MODEL))],
        out_specs=row_spec(D_MODEL),
        out_shape=jax.ShapeDtypeStruct((tokens, D_MODEL), F32),
        compiler_params=pltpu.CompilerParams(
            dimension_semantics=("parallel",),
            vmem_limit_bytes=_vmem_limit(weight_bytes + tile_bytes + temp_bytes)),
        name="mix_out",
    )(x2d, ma, sgb, *outs, *lses, w_b, w_out, g_ffn, w_gate, w_up, w_down, g_final)


def kernel(x, norm_mix_g, w_in, gmlp_ln_g, gmlp_ln_b, gmlp_ws, gmlp_bs, w_branch_gmlp,
           w_branch_attn, w_out, norm_ffn_g, w_ffn_gate, w_ffn_up, w_ffn_down, norm_final_g):
    batch, seq, d_model = x.shape
    assert w_in.shape[0] == 1 and d_model == D_MODEL and w_in.shape[2] == IN_WIDTH
    for window, dilation in ATTN_PATTERNS:
        assert window == 2 * HALF * dilation
    tokens = batch * seq
    h = x.reshape(tokens, d_model)
    row = lambda v: v.reshape(1, -1).astype(F32)
    stage1 = _mix_in(
        h, row(norm_mix_g[0]), w_in[0].astype(BF16), row(gmlp_ln_g[0]), row(gmlp_ln_b[0]),
        gmlp_ws[0].astype(BF16), gmlp_bs[0].astype(F32).T, w_branch_gmlp[0].astype(BF16))
    qs, ks, vs = stage1[0:3], stage1[3:6], stage1[6:9]
    ma, sgb = stage1[9], stage1[10]
    outs, lses = [], []
    for gi, (_, dilation) in enumerate(ATTN_PATTERNS):
        o, lse = _attention_group(gi, dilation, qs[gi], ks[gi], vs[gi], batch, seq)
        outs.append(o)
        lses.append(lse)
    out = _mix_out(h, ma, sgb, outs, lses, w_branch_attn[0].astype(BF16), w_out[0].astype(BF16),
                   row(norm_ffn_g[0]), w_ffn_gate[0].astype(BF16), w_ffn_up[0].astype(BF16),
                   w_ffn_down[0].astype(BF16), row(norm_final_g))
    return out.reshape(batch, seq, d_model)
```

```python
import functools
import math

import jax
import jax.numpy as jnp
from jax import lax
from jax.experimental import pallas as pl
from jax.experimental.pallas import tpu as pltpu

F32 = jnp.float32
BF16 = jnp.bfloat16

D_MODEL = 1024
NORM_EPS = 1e-6
NEG_INF = -1e30

CHUNK = 128
GMLP_WIDTH = D_MODEL
GMLP_GROUPS = 8

ATTN_PATTERNS = ((128, 1), (512, 4), (2048, 16))
N_GROUPS = len(ATTN_PATTERNS)
HEADS = 8
HEAD_DIM = 64
GROUP_WIDTH = HEADS * HEAD_DIM
QKV_WIDTH = N_GROUPS * GROUP_WIDTH
Q_BLOCK = 128
HALF = 64
KEY_WIN = Q_BLOCK + 2 * HALF
HEAD_PAIR = 2 * HEAD_DIM

OFF_U = 0
OFF_V = OFF_U + GMLP_WIDTH
OFF_Q = OFF_V + GMLP_WIDTH
OFF_K = OFF_Q + QKV_WIDTH
OFF_VA = OFF_K + QKV_WIDTH
OFF_GA = OFF_VA + QKV_WIDTH
OFF_GB = OFF_GA + D_MODEL
IN_WIDTH = OFF_GB + D_MODEL

LANES = 128
BF16_SUBLANES = 16
V7X_VMEM_BYTES = 64 * 1024 * 1024
MIX_IN_ROWS = 256
MIX_OUT_ROWS = 256
ATTN_ROWS = 512


def _vmem_limit(nbytes):
    return int(min(nbytes * 3 // 2, V7X_VMEM_BYTES - (4 << 20)))


def _const_spec(shape):
    zeros = (0,) * len(shape)
    return pl.BlockSpec(shape, lambda *_: zeros, pipeline_mode=pl.Buffered(1))


def _rms_norm(x, g):
    return x * lax.rsqrt(jnp.mean(x * x, axis=-1, keepdims=True) + NORM_EPS) * g


def _dot(a, b):
    return jnp.dot(a, b, preferred_element_type=F32)


def _gelu(x):
    return 0.5 * x * (1.0 + lax.erf(x * math.sqrt(0.5)))


def _store_subsequence_major(val, out_ref, dilation, slab_scr):
    if dilation == 1:
        out_ref[...] = val.astype(out_ref.dtype)
        return
    n = val.shape[0] // dilation
    for s in range(GROUP_WIDTH // LANES):
        slab_scr[s] = val[:, s * LANES:(s + 1) * LANES]
    for r in range(dilation):
        for s in range(GROUP_WIDTH // LANES):
            lo = r * GROUP_WIDTH + s * LANES
            out_ref[:, lo:lo + LANES] = slab_scr[s, pl.ds(r, n, stride=dilation), :].astype(out_ref.dtype)


def _mix_in_kernel(x_ref, g_ref, win_ref, lng_ref, lnb_ref, ws_ref, bst_ref, wa_ref,
                   q0_ref, q1_ref, q2_ref, k0_ref, k1_ref, k2_ref, v0_ref, v1_ref, v2_ref,
                   ma_ref, sgb_ref, y_scr, slab_scr):
    rows = x_ref.shape[0]
    xn = _rms_norm(x_ref[...], g_ref[...]).astype(BF16)

    def proj(off, width):
        return _dot(xn, win_ref[:, off:off + width])

    scale = 1.0 / math.sqrt(HEAD_DIM)
    for gi, (q_ref, k_ref, v_ref) in enumerate(
            ((q0_ref, k0_ref, v0_ref), (q1_ref, k1_ref, v1_ref), (q2_ref, k2_ref, v2_ref))):
        off = gi * GROUP_WIDTH
        dilation = ATTN_PATTERNS[gi][1]
        _store_subsequence_major(proj(OFF_Q + off, GROUP_WIDTH) * scale, q_ref, dilation, slab_scr)
        _store_subsequence_major(proj(OFF_K + off, GROUP_WIDTH), k_ref, dilation, slab_scr)
        _store_subsequence_major(proj(OFF_VA + off, GROUP_WIDTH), v_ref, dilation, slab_scr)

    sgb_ref[...] = jax.nn.sigmoid(proj(OFF_GB, D_MODEL))

    gu = _gelu(proj(OFF_U, GMLP_WIDTH))
    gv = _gelu(proj(OFF_V, GMLP_WIDTH))
    mean = jnp.mean(gv, axis=-1, keepdims=True)
    cen = gv - mean
    var = jnp.mean(cen * cen, axis=-1, keepdims=True)
    vn = (cen * lax.rsqrt(var + NORM_EPS) * lng_ref[...] + lnb_ref[...]).astype(BF16)

    for c in range(rows // CHUNK):
        rs = slice(c * CHUNK, (c + 1) * CHUNK)
        for g in range(GMLP_GROUPS):
            cs = slice(g * CHUNK, (g + 1) * CHUNK)
            mixed = _dot(ws_ref[g], vn[rs, cs]) + bst_ref[:, g:g + 1]
            y_scr[rs, cs] = (gu[rs, cs] * mixed).astype(BF16)

    ya = _dot(y_scr[...], wa_ref[...])
    ma_ref[...] = jax.nn.sigmoid(proj(OFF_GA, D_MODEL)) * ya


def _mix_in(x2d, norm_g, w_in, ln_g, ln_b, ws, bs_t, w_a):
    tokens = x2d.shape[0]
    rows = MIX_IN_ROWS
    assert tokens % rows == 0 and rows % CHUNK == 0
    row_spec = lambda width: pl.BlockSpec((rows, width), lambda i: (i, 0))
    dilations = [d for _, d in ATTN_PATTERNS]
    assert all(rows % (BF16_SUBLANES * d) == 0 for d in dilations)
    qkv_specs = [pl.BlockSpec((rows // d, d * GROUP_WIDTH), lambda i: (i, 0)) for d in dilations] * 3
    qkv_shapes = [jax.ShapeDtypeStruct((tokens // d, d * GROUP_WIDTH), BF16) for d in dilations] * 3
    f32_shape = jax.ShapeDtypeStruct((tokens, D_MODEL), F32)
    weight_bytes = 2 * (w_in.size + w_a.size + ws.size)
    tile_bytes = 2 * rows * (4 * D_MODEL + 9 * 2 * GROUP_WIDTH + 2 * 4 * D_MODEL)
    temp_bytes = rows * D_MODEL * (2 + 6 * 4) + rows * GROUP_WIDTH * 4
    return pl.pallas_call(
        _mix_in_kernel,
        grid=(tokens // rows,),
        in_specs=[
            row_spec(D_MODEL),
            _const_spec((1, D_MODEL)),
            _const_spec((D_MODEL, IN_WIDTH)),
            _const_spec((1, GMLP_WIDTH)),
            _const_spec((1, GMLP_WIDTH)),
            _const_spec((GMLP_GROUPS, CHUNK, CHUNK)),
            _const_spec((CHUNK, GMLP_GROUPS)),
            _const_spec((GMLP_WIDTH, D_MODEL)),
        ],
        out_specs=qkv_specs + [row_spec(D_MODEL)] * 2,
        out_shape=qkv_shapes + [f32_shape] * 2,
        scratch_shapes=[pltpu.VMEM((rows, GMLP_WIDTH), BF16),
                        pltpu.VMEM((GROUP_WIDTH // LANES, rows, LANES), F32)],
        compiler_params=pltpu.CompilerParams(
            dimension_semantics=("parallel",),
            vmem_limit_bytes=_vmem_limit(weight_bytes + tile_bytes + temp_bytes)),
        name="mix_in",
    )(x2d, norm_g, w_in, ln_g, ln_b, ws, bs_t, w_a)


def _attn_kernel(q_ref, k_ref, kl_ref, kr_ref, v_ref, vl_ref, vr_ref, bias_ref,
                 o_ref, lse_ref):
    rows = q_ref.shape[0]
    n_blk = rows // Q_BLOCK
    first = pl.program_id(2) == 0
    last = pl.program_id(2) == pl.num_programs(2) - 1
    lane = lax.broadcasted_iota(jnp.int32, (Q_BLOCK, HEAD_PAIR), 1)
    low_half = lane < HEAD_DIM

    for j in range(n_blk):
        lo = j * Q_BLOCK - HALF
        if j == 0:
            edge = jnp.where(first, 1, 0)
        elif j == n_blk - 1:
            edge = jnp.where(last, 2, 0)
        else:
            edge = 0
        for p in range(HEADS // 2):
            cs = slice(p * HEAD_PAIR, (p + 1) * HEAD_PAIR)

            def window(main_ref, left_ref, right_ref):
                if j == 0:
                    return jnp.concatenate([left_ref[:, cs], main_ref[0:lo + KEY_WIN, cs]], axis=0)
                if j == n_blk - 1:
                    return jnp.concatenate([main_ref[lo:rows, cs], right_ref[:, cs]], axis=0)
                return main_ref[lo:lo + KEY_WIN, cs]

            k_win = window(k_ref, kl_ref, kr_ref)
            v_win = window(v_ref, vl_ref, vr_ref)
            q_pair = q_ref[j * Q_BLOCK:(j + 1) * Q_BLOCK, cs]
            outs, lses = [], []
            for e in range(2):
                own_lanes = low_half if e == 0 else jnp.logical_not(low_half)
                q_head = jnp.where(own_lanes, q_pair, jnp.zeros_like(q_pair))
                s = lax.dot_general(q_head, k_win, (((1,), (1,)), ((), ())),
                                    preferred_element_type=F32)
                s = s + bias_ref[edge, 2 * p + e]
                m = jnp.max(s, axis=-1, keepdims=True)
                prob = jnp.exp(s - m)
                denom = jnp.sum(prob, axis=-1, keepdims=True)
                outs.append(_dot(prob.astype(BF16), v_win) / denom)
                lses.append(m + jnp.log(denom))
            rs = slice(j * Q_BLOCK, (j + 1) * Q_BLOCK)
            o_ref[rs, cs] = jnp.where(low_half, outs[0], outs[1]).astype(o_ref.dtype)
            lse_ref[rs, cs] = jnp.where(low_half, lses[0], lses[1])


def _attn_bias(group, dilation):
    n = N_GROUPS * HEADS
    idx = jnp.arange(1, n + 1, dtype=F32)
    slopes = jnp.exp2(-8.0 * idx / n).reshape(N_GROUPS, HEADS)[group]
    qi = jnp.arange(Q_BLOCK)[:, None]
    kj = jnp.arange(KEY_WIN)[None, :]
    rel = jnp.abs(kj - HALF - qi)
    dist = (rel * dilation).astype(F32)
    band = rel <= HALF
    in_seq = jnp.stack([jnp.ones_like(kj, dtype=bool), kj >= HALF, kj < KEY_WIN - HALF])
    valid = band[None] & in_seq
    score_bias = -slopes[:, None, None] * dist[None]
    return jnp.where(valid[:, None], score_bias[None], NEG_INF)


def _attention_group(group, dilation, q, k, v, batch, seq):
    length = seq // dilation
    rows = min(ATTN_ROWS, length)
    assert length % rows == 0 and rows % Q_BLOCK == 0 and rows >= 2 * Q_BLOCK
    n_tiles = length // rows
    halo_per_tile = rows // HALF
    n_halo = length // HALF
    width = dilation * GROUP_WIDTH
    view = lambda t: t.reshape(batch, length, width)

    main = pl.BlockSpec((None, rows, GROUP_WIDTH), lambda b, r, t: (b, t, r))
    left = pl.BlockSpec((None, HALF, GROUP_WIDTH),
                        lambda b, r, t: (b, jnp.maximum(t * halo_per_tile - 1, 0), r))
    right = pl.BlockSpec((None, HALF, GROUP_WIDTH),
                         lambda b, r, t: (b, jnp.minimum((t + 1) * halo_per_tile, n_halo - 1), r))
    bias = _attn_bias(group, dilation)
    tile_bytes = 2 * (3 * rows * GROUP_WIDTH * 2 + 4 * HALF * GROUP_WIDTH * 2
                      + rows * GROUP_WIDTH * (2 + 4))
    out, lse = pl.pallas_call(
        _attn_kernel,
        grid=(batch, dilation, n_tiles),
        in_specs=[main, main, left, right, main, left, right,
                  _const_spec(bias.shape)],
        out_specs=[main, main],
        out_shape=[jax.ShapeDtypeStruct((batch, length, width), BF16),
                   jax.ShapeDtypeStruct((batch, length, width), F32)],
        compiler_params=pltpu.CompilerParams(
            dimension_semantics=("parallel", "parallel", "parallel"),
            vmem_limit_bytes=_vmem_limit(tile_bytes + bias.size * 4 + (8 << 20))),
        name=f"attn_dil{dilation}",
    )(view(q), view(k), view(k), view(k), view(v), view(v), view(v), bias)
    return out.reshape(batch * length, width), lse.reshape(batch * length, width)


def _load_token_major(ref, dilation, slab_scr):
    if dilation == 1:
        return ref[...].astype(F32)
    n = ref.shape[0]
    n_slabs = GROUP_WIDTH // LANES
    for r in range(dilation):
        for s in range(n_slabs):
            lo = r * GROUP_WIDTH + s * LANES
            slab_scr[s, pl.ds(r, n, stride=dilation), :] = ref[:, lo:lo + LANES].astype(F32)
    return jnp.concatenate([slab_scr[s] for s in range(n_slabs)], axis=1)


def _mix_out_kernel(x_ref, ma_ref, sgb_ref, o0_ref, o1_ref, o2_ref, l0_ref, l1_ref, l2_ref,
                    wb_ref, wo_ref, gf_ref, wg_ref, wu_ref, wd_ref, gl_ref, out_ref, slab_scr):
    dil = [d for _, d in ATTN_PATTERNS]
    l0, l1, l2 = (_load_token_major(ref, d, slab_scr.at[i])
                  for i, (ref, d) in enumerate(zip((l0_ref, l1_ref, l2_ref), dil)))
    o0, o1, o2 = (_load_token_major(ref, d, slab_scr.at[N_GROUPS + i])
                  for i, (ref, d) in enumerate(zip((o0_ref, o1_ref, o2_ref), dil)))
    top = jnp.maximum(jnp.maximum(l0, l1), l2)
    e0, e1, e2 = jnp.exp(l0 - top), jnp.exp(l1 - top), jnp.exp(l2 - top)
    y = (e0 * o0 + e1 * o1 + e2 * o2) / (e0 + e1 + e2)
    merged = ma_ref[...] + sgb_ref[...] * _dot(y.astype(BF16), wb_ref[...])
    h = x_ref[...] + _dot(merged.astype(BF16), wo_ref[...])
    hn = _rms_norm(h, gf_ref[...]).astype(BF16)
    act = jax.nn.silu(_dot(hn, wg_ref[...])) * _dot(hn, wu_ref[...])
    h = h + _dot(act.astype(BF16), wd_ref[...])
    out_ref[...] = _rms_norm(h, gl_ref[...])


def _mix_out(x2d, ma, sgb, outs, lses, w_b, w_out, g_ffn, w_gate, w_up, w_down, g_final):
    tokens = x2d.shape[0]
    rows = MIX_OUT_ROWS
    assert tokens % rows == 0
    d_ff = w_gate.shape[1]
    row_spec = lambda width: pl.BlockSpec((rows, width), lambda i: (i, 0))
    dilations = [d for _, d in ATTN_PATTERNS]
    assert all(rows % (BF16_SUBLANES * d) == 0 for d in dilations)
    group_specs = [pl.BlockSpec((rows // d, d * GROUP_WIDTH), lambda i: (i, 0)) for d in dilations]
    weights = (w_b, w_out, w_gate, w_up, w_down)
    weight_bytes = 2 * sum(w.size for w in weights)
    slab_shape = (2 * N_GROUPS, GROUP_WIDTH // LANES, rows, LANES)
    tile_bytes = 2 * rows * (4 * 4 * D_MODEL + 3 * GROUP_WIDTH * (2 + 4))
    temp_bytes = rows * (3 * 4 * d_ff + 4 * 4 * D_MODEL) + 4 * math.prod(slab_shape)
    return pl.pallas_call(
        _mix_out_kernel,
        grid=(tokens // rows,),
        scratch_shapes=[pltpu.VMEM(slab_shape, F32)],
        in_specs=[row_spec(D_MODEL)] * 3 + group_specs * 2 + [
            _const_spec(w_b.shape), _const_spec(w_out.shape), _const_spec((1, D_MODEL)),
            _const_spec(w_gate.shape), _const_spec(w_up.shape), _const_spec(w_down.shape),
            _const_spec((1, D_MODEL))],
        out_specs=row_spec(D_MODEL),
        out_shape=jax.ShapeDtypeStruct((tokens, D_MODEL), F32),
        compiler_params=pltpu.CompilerParams(
            dimension_semantics=("parallel",),
            vmem_limit_bytes=_vmem_limit(weight_bytes + tile_bytes + temp_bytes)),
        name="mix_out",
    )(x2d, ma, sgb, *outs, *lses, w_b, w_out, g_ffn, w_gate, w_up, w_down, g_final)


def kernel(x, norm_mix_g, w_in, gmlp_ln_g, gmlp_ln_b, gmlp_ws, gmlp_bs, w_branch_gmlp,
           w_branch_attn, w_out, norm_ffn_g, w_ffn_gate, w_ffn_up, w_ffn_down, norm_final_g):
    batch, seq, d_model = x.shape
    assert w_in.shape[0] == 1 and d_model == D_MODEL and w_in.shape[2] == IN_WIDTH
    for window, dilation in ATTN_PATTERNS:
        assert window == 2 * HALF * dilation
    tokens = batch * seq
    h = x.reshape(tokens, d_model)
    row = lambda v: v.reshape(1, -1).astype(F32)
    stage1 = _mix_in(
        h, row(norm_mix_g[0]), w_in[0].astype(BF16), row(gmlp_ln_g[0]), row(gmlp_ln_b[0]),
        gmlp_ws[0].astype(BF16), gmlp_bs[0].astype(F32).T, w_branch_gmlp[0].astype(BF16))
    qs, ks, vs = stage1[0:3], stage1[3:6], stage1[6:9]
    ma, sgb = stage1[9], stage1[10]
    outs, lses = [], []
    for gi, (_, dilation) in enumerate(ATTN_PATTERNS):
        o, lse = _attention_group(gi, dilation, qs[gi], ks[gi], vs[gi], batch, seq)
        outs.append(o)
        lses.append(lse)
    out = _mix_out(h, ma, sgb, outs, lses, w_branch_attn[0].astype(BF16), w_out[0].astype(BF16),
                   row(norm_ffn_g[0]), w_ffn_gate[0].astype(BF16), w_ffn_up[0].astype(BF16),
                   w_ffn_down[0].astype(BF16), row(norm_final_g))
    return out.reshape(batch, seq, d_model)
```

```python
import functools
import math

import jax
import jax.numpy as jnp
from jax import lax
from jax.experimental import pallas as pl
from jax.experimental.pallas import tpu as pltpu

F32 = jnp.float32
BF16 = jnp.bfloat16

D_MODEL = 1024
NORM_EPS = 1e-6
NEG_INF = -1e30
LOG2_E = math.log2(math.e)

CHUNK = 128
GMLP_WIDTH = D_MODEL
GMLP_GROUPS = 8

ATTN_PATTERNS = ((128, 1), (512, 4), (2048, 16))
N_GROUPS = len(ATTN_PATTERNS)
HEADS = 8
HEAD_DIM = 64
GROUP_WIDTH = HEADS * HEAD_DIM
QKV_WIDTH = N_GROUPS * GROUP_WIDTH
Q_BLOCK = 128
HALF = 64
KEY_WIN = Q_BLOCK + 2 * HALF
HEAD_PAIR = 2 * HEAD_DIM

OFF_U = 0
OFF_V = OFF_U + GMLP_WIDTH
OFF_Q = OFF_V + GMLP_WIDTH
OFF_K = OFF_Q + QKV_WIDTH
OFF_VA = OFF_K + QKV_WIDTH
OFF_GA = OFF_VA + QKV_WIDTH
OFF_GB = OFF_GA + D_MODEL
IN_WIDTH = OFF_GB + D_MODEL

LANES = 128
BF16_SUBLANES = 16
V7X_VMEM_BYTES = 64 * 1024 * 1024
MIX_IN_ROWS = 256
MIX_OUT_ROWS = 256
ATTN_ROWS = 512


def _vmem_limit(nbytes):
    return int(min(nbytes * 3 // 2, V7X_VMEM_BYTES - (4 << 20)))


def _const_spec(shape):
    zeros = (0,) * len(shape)
    return pl.BlockSpec(shape, lambda *_: zeros, pipeline_mode=pl.Buffered(1))


def _rms_norm(x, g):
    return x * lax.rsqrt(jnp.mean(x * x, axis=-1, keepdims=True) + NORM_EPS) * g


def _dot(a, b):
    return jnp.dot(a, b, preferred_element_type=F32)


def _gelu(x):
    return 0.5 * x * (1.0 + lax.erf(x * math.sqrt(0.5)))


def _store_subsequence_major(val, out_ref, dilation, slab_scr):
    if dilation == 1:
        out_ref[...] = val.astype(out_ref.dtype)
        return
    n = val.shape[0] // dilation
    for s in range(GROUP_WIDTH // LANES):
        slab_scr[s] = val[:, s * LANES:(s + 1) * LANES]
    for r in range(dilation):
        for s in range(GROUP_WIDTH // LANES):
            lo = r * GROUP_WIDTH + s * LANES
            out_ref[:, lo:lo + LANES] = slab_scr[s, pl.ds(r, n, stride=dilation), :].astype(out_ref.dtype)


def _mix_in_kernel(x_ref, g_ref, win_ref, lng_ref, lnb_ref, ws_ref, bst_ref, wa_ref,
                   q0_ref, q1_ref, q2_ref, k0_ref, k1_ref, k2_ref, v0_ref, v1_ref, v2_ref,
                   ma_ref, sgb_ref, y_scr, slab_scr):
    rows = x_ref.shape[0]
    xn = _rms_norm(x_ref[...], g_ref[...]).astype(BF16)

    def proj(off, width):
        return _dot(xn, win_ref[:, off:off + width])

    scale = LOG2_E / math.sqrt(HEAD_DIM)
    for gi, (q_ref, k_ref, v_ref) in enumerate(
            ((q0_ref, k0_ref, v0_ref), (q1_ref, k1_ref, v1_ref), (q2_ref, k2_ref, v2_ref))):
        off = gi * GROUP_WIDTH
        dilation = ATTN_PATTERNS[gi][1]
        _store_subsequence_major(proj(OFF_Q + off, GROUP_WIDTH) * scale, q_ref, dilation, slab_scr)
        _store_subsequence_major(proj(OFF_K + off, GROUP_WIDTH), k_ref, dilation, slab_scr)
        _store_subsequence_major(proj(OFF_VA + off, GROUP_WIDTH), v_ref, dilation, slab_scr)

    sgb_ref[...] = jax.nn.sigmoid(proj(OFF_GB, D_MODEL))

    gu = _gelu(proj(OFF_U, GMLP_WIDTH))
    gv = _gelu(proj(OFF_V, GMLP_WIDTH))
    mean = jnp.mean(gv, axis=-1, keepdims=True)
    cen = gv - mean
    var = jnp.mean(cen * cen, axis=-1, keepdims=True)
    vn = (cen * lax.rsqrt(var + NORM_EPS) * lng_ref[...] + lnb_ref[...]).astype(BF16)

    for c in range(rows // CHUNK):
        rs = slice(c * CHUNK, (c + 1) * CHUNK)
        for g in range(GMLP_GROUPS):
            cs = slice(g * CHUNK, (g + 1) * CHUNK)
            mixed = _dot(ws_ref[g], vn[rs, cs]) + bst_ref[:, g:g + 1]
            y_scr[rs, cs] = (gu[rs, cs] * mixed).astype(BF16)

    ya = _dot(y_scr[...], wa_ref[...])
    ma_ref[...] = jax.nn.sigmoid(proj(OFF_GA, D_MODEL)) * ya


def _mix_in(x2d, norm_g, w_in, ln_g, ln_b, ws, bs_t, w_a):
    tokens = x2d.shape[0]
    rows = MIX_IN_ROWS
    assert tokens % rows == 0 and rows % CHUNK == 0
    row_spec = lambda width: pl.BlockSpec((rows, width), lambda i: (i, 0))
    dilations = [d for _, d in ATTN_PATTERNS]
    assert all(rows % (BF16_SUBLANES * d) == 0 for d in dilations)
    qkv_specs = [pl.BlockSpec((rows // d, d * GROUP_WIDTH), lambda i: (i, 0)) for d in dilations] * 3
    qkv_shapes = [jax.ShapeDtypeStruct((tokens // d, d * GROUP_WIDTH), BF16) for d in dilations] * 3
    f32_shape = jax.ShapeDtypeStruct((tokens, D_MODEL), F32)
    weight_bytes = 2 * (w_in.size + w_a.size + ws.size)
    tile_bytes = 2 * rows * (4 * D_MODEL + 9 * 2 * GROUP_WIDTH + 2 * 4 * D_MODEL)
    temp_bytes = rows * D_MODEL * (2 + 6 * 4) + rows * GROUP_WIDTH * 4
    return pl.pallas_call(
        _mix_in_kernel,
        grid=(tokens // rows,),
        in_specs=[
            row_spec(D_MODEL),
            _const_spec((1, D_MODEL)),
            _const_spec((D_MODEL, IN_WIDTH)),
            _const_spec((1, GMLP_WIDTH)),
            _const_spec((1, GMLP_WIDTH)),
            _const_spec((GMLP_GROUPS, CHUNK, CHUNK)),
            _const_spec((CHUNK, GMLP_GROUPS)),
            _const_spec((GMLP_WIDTH, D_MODEL)),
        ],
        out_specs=qkv_specs + [row_spec(D_MODEL)] * 2,
        out_shape=qkv_shapes + [f32_shape] * 2,
        scratch_shapes=[pltpu.VMEM((rows, GMLP_WIDTH), BF16),
                        pltpu.VMEM((GROUP_WIDTH // LANES, rows, LANES), F32)],
        compiler_params=pltpu.CompilerParams(
            dimension_semantics=("parallel",),
            vmem_limit_bytes=_vmem_limit(weight_bytes + tile_bytes + temp_bytes)),
        name="mix_in",
    )(x2d, norm_g, w_in, ln_g, ln_b, ws, bs_t, w_a)


def _attn_kernel(q_ref, k_ref, kl_ref, kr_ref, v_ref, vl_ref, vr_ref, bias_ref,
                 o_ref, lse_ref):
    rows = q_ref.shape[0]
    n_blk = rows // Q_BLOCK
    first = pl.program_id(2) == 0
    last = pl.program_id(2) == pl.num_programs(2) - 1
    lane = lax.broadcasted_iota(jnp.int32, (Q_BLOCK, HEAD_PAIR), 1)
    low_half = lane < HEAD_DIM
    ones_cols = jnp.ones((KEY_WIN, LANES), BF16)

    for j in range(n_blk):
        lo = j * Q_BLOCK - HALF
        if j == 0:
            edge = jnp.where(first, 1, 0)
        elif j == n_blk - 1:
            edge = jnp.where(last, 2, 0)
        else:
            edge = 0
        for p in range(HEADS // 2):
            cs = slice(p * HEAD_PAIR, (p + 1) * HEAD_PAIR)

            def window(main_ref, left_ref, right_ref):
                if j == 0:
                    return jnp.concatenate([left_ref[:, cs], main_ref[0:lo + KEY_WIN, cs]], axis=0)
                if j == n_blk - 1:
                    return jnp.concatenate([main_ref[lo:rows, cs], right_ref[:, cs]], axis=0)
                return main_ref[lo:lo + KEY_WIN, cs]

            k_win = window(k_ref, kl_ref, kr_ref)
            v_aug = jnp.concatenate([window(v_ref, vl_ref, vr_ref), ones_cols], axis=1)
            q_pair = q_ref[j * Q_BLOCK:(j + 1) * Q_BLOCK, cs]
            accs, maxes = [], []
            for e in range(2):
                own_lanes = low_half if e == 0 else jnp.logical_not(low_half)
                q_head = jnp.where(own_lanes, q_pair, jnp.zeros_like(q_pair))
                s = lax.dot_general(q_head, k_win, (((1,), (1,)), ((), ())),
                                    preferred_element_type=F32)
                s = s + bias_ref[edge, 2 * p + e]
                m = jnp.max(s, axis=-1, keepdims=True)
                prob = jnp.exp2(s - m).astype(BF16)
                accs.append(_dot(prob, v_aug))
                maxes.append(m)
            numer = jnp.where(low_half, accs[0][:, :LANES], accs[1][:, :LANES])
            denom = jnp.where(low_half, accs[0][:, LANES:], accs[1][:, LANES:])
            row_max = jnp.where(low_half, maxes[0], maxes[1])
            rs = slice(j * Q_BLOCK, (j + 1) * Q_BLOCK)
            o_ref[rs, cs] = (numer / denom).astype(o_ref.dtype)
            lse_ref[rs, cs] = row_max + jnp.log2(denom)


def _attn_bias(group, dilation):
    n = N_GROUPS * HEADS
    idx = jnp.arange(1, n + 1, dtype=F32)
    slopes = jnp.exp2(-8.0 * idx / n).reshape(N_GROUPS, HEADS)[group]
    qi = jnp.arange(Q_BLOCK)[:, None]
    kj = jnp.arange(KEY_WIN)[None, :]
    rel = jnp.abs(kj - HALF - qi)
    dist = (rel * dilation).astype(F32)
    band = rel <= HALF
    in_seq = jnp.stack([jnp.ones_like(kj, dtype=bool), kj >= HALF, kj < KEY_WIN - HALF])
    valid = band[None] & in_seq
    score_bias = -LOG2_E * slopes[:, None, None] * dist[None]
    return jnp.where(valid[:, None], score_bias[None], NEG_INF)


def _attention_group(group, dilation, q, k, v, batch, seq):
    length = seq // dilation
    rows = min(ATTN_ROWS, length)
    assert length % rows == 0 and rows % Q_BLOCK == 0 and rows >= 2 * Q_BLOCK
    n_tiles = length // rows
    halo_per_tile = rows // HALF
    n_halo = length // HALF
    width = dilation * GROUP_WIDTH
    view = lambda t: t.reshape(batch, length, width)

    main = pl.BlockSpec((None, rows, GROUP_WIDTH), lambda b, r, t: (b, t, r))
    left = pl.BlockSpec((None, HALF, GROUP_WIDTH),
                        lambda b, r, t: (b, jnp.maximum(t * halo_per_tile - 1, 0), r))
    right = pl.BlockSpec((None, HALF, GROUP_WIDTH),
                         lambda b, r, t: (b, jnp.minimum((t + 1) * halo_per_tile, n_halo - 1), r))
    bias = _attn_bias(group, dilation)
    tile_bytes = 2 * (3 * rows * GROUP_WIDTH * 2 + 4 * HALF * GROUP_WIDTH * 2
                      + rows * GROUP_WIDTH * (2 + 4))
    out, lse = pl.pallas_call(
        _attn_kernel,
        grid=(batch, dilation, n_tiles),
        in_specs=[main, main, left, right, main, left, right,
                  _const_spec(bias.shape)],
        out_specs=[main, main],
        out_shape=[jax.ShapeDtypeStruct((batch, length, width), BF16),
                   jax.ShapeDtypeStruct((batch, length, width), F32)],
        compiler_params=pltpu.CompilerParams(
            dimension_semantics=("parallel", "parallel", "parallel"),
            vmem_limit_bytes=_vmem_limit(tile_bytes + bias.size * 4 + (8 << 20))),
        name=f"attn_dil{dilation}",
    )(view(q), view(k), view(k), view(k), view(v), view(v), view(v), bias)
    return out.reshape(batch * length, width), lse.reshape(batch * length, width)


def _load_token_major(ref, dilation, slab_scr):
    if dilation == 1:
        return ref[...].astype(F32)
    n = ref.shape[0]
    n_slabs = GROUP_WIDTH // LANES
    for r in range(dilation):
        for s in range(n_slabs):
            lo = r * GROUP_WIDTH + s * LANES
            slab_scr[s, pl.ds(r, n, stride=dilation), :] = ref[:, lo:lo + LANES].astype(F32)
    return jnp.concatenate([slab_scr[s] for s in range(n_slabs)], axis=1)


def _mix_out_kernel(x_ref, ma_ref, sgb_ref, o0_ref, o1_ref, o2_ref, l0_ref, l1_ref, l2_ref,
                    wb_ref, wo_ref, gf_ref, wg_ref, wu_ref, wd_ref, gl_ref, out_ref, slab_scr):
    dil = [d for _, d in ATTN_PATTERNS]
    l0, l1, l2 = (_load_token_major(ref, d, slab_scr.at[i])
                  for i, (ref, d) in enumerate(zip((l0_ref, l1_ref, l2_ref), dil)))
    o0, o1, o2 = (_load_token_major(ref, d, slab_scr.at[N_GROUPS + i])
                  for i, (ref, d) in enumerate(zip((o0_ref, o1_ref, o2_ref), dil)))
    top = jnp.maximum(jnp.maximum(l0, l1), l2)
    e0, e1, e2 = jnp.exp2(l0 - top), jnp.exp2(l1 - top), jnp.exp2(l2 - top)
    y = (e0 * o0 + e1 * o1 + e2 * o2) / (e0 + e1 + e2)
    merged = ma_ref[...] + sgb_ref[...] * _dot(y.astype(BF16), wb_ref[...])
    h = x_ref[...] + _dot(merged.astype(BF16), wo_ref[...])
    hn = _rms_norm(h, gf_ref[...]).astype(BF16)
    act = jax.nn.silu(_dot(hn, wg_ref[...])) * _dot(hn, wu_ref[...])
    h = h + _dot(act.astype(BF16), wd_ref[...])
    out_ref[...] = _rms_norm(h, gl_ref[...])


def _mix_out(x2d, ma, sgb, outs, lses, w_b, w_out, g_ffn, w_gate, w_up, w_down, g_final):
    tokens = x2d.shape[0]
    rows = MIX_OUT_ROWS
    assert tokens % rows == 0
    d_ff = w_gate.shape[1]
    row_spec = lambda width: pl.BlockSpec((rows, width), lambda i: (i, 0))
    dilations = [d for _, d in ATTN_PATTERNS]
    assert all(rows % (BF16_SUBLANES * d) == 0 for d in dilations)
    group_specs = [pl.BlockSpec((rows // d, d * GROUP_WIDTH), lambda i: (i, 0)) for d in dilations]
    weights = (w_b, w_out, w_gate, w_up, w_down)
    weight_bytes = 2 * sum(w.size for w in weights)
    slab_shape = (2 * N_GROUPS, GROUP_WIDTH // LANES, rows, LANES)
    tile_bytes = 2 * rows * (4 * 4 * D_MODEL + 3 * GROUP_WIDTH * (2 + 4))
    temp_bytes = rows * (3 * 4 * d_ff + 4 * 4 * D_MODEL) + 4 * math.prod(slab_shape)
    return pl.pallas_call(
        _mix_out_kernel,
        grid=(tokens // rows,),
        scratch_shapes=[pltpu.VMEM(slab_shape, F32)],
        in_specs=[row_spec(D_MODEL)] * 3 + group_specs * 2 + [
            _const_spec(w_b.shape), _const_spec(w_out.shape), _const_spec((1, D_MODEL)),
            _const_spec(w_gate.shape), _const_spec(w_up.shape), _const_spec(w_down.shape),
            _const_spec((1, D_MODEL))],
        out_specs=row_spec(D_MODEL),
        out_shape=jax.ShapeDtypeStruct((tokens, D_MODEL), F32),
        compiler_params=pltpu.CompilerParams(
            dimension_semantics=("parallel",),
            vmem_limit_bytes=_vmem_limit(weight_bytes + tile_bytes + temp_bytes)),
        name="mix_out",
    )(x2d, ma, sgb, *outs, *lses, w_b, w_out, g_ffn, w_gate, w_up, w_down, g_final)


def kernel(x, norm_mix_g, w_in, gmlp_ln_g, gmlp_ln_b, gmlp_ws, gmlp_bs, w_branch_gmlp,
           w_branch_attn, w_out, norm_ffn_g, w_ffn_gate, w_ffn_up, w_ffn_down, norm_final_g):
    batch, seq, d_model = x.shape
    assert w_in.shape[0] == 1 and d_model == D_MODEL and w_in.shape[2] == IN_WIDTH
    for window, dilation in ATTN_PATTERNS:
        assert window == 2 * HALF * dilation
    tokens = batch * seq
    h = x.reshape(tokens, d_model)
    row = lambda v: v.reshape(1, -1).astype(F32)
    stage1 = _mix_in(
        h, row(norm_mix_g[0]), w_in[0].astype(BF16), row(gmlp_ln_g[0]), row(gmlp_ln_b[0]),
        gmlp_ws[0].astype(BF16), gmlp_bs[0].astype(F32).T, w_branch_gmlp[0].astype(BF16))
    qs, ks, vs = stage1[0:3], stage1[3:6], stage1[6:9]
    ma, sgb = stage1[9], stage1[10]
    outs, lses = [], []
    for gi, (_, dilation) in enumerate(ATTN_PATTERNS):
        o, lse = _attention_group(gi, dilation, qs[gi], ks[gi], vs[gi], batch, seq)
        outs.append(o)
        lses.append(lse)
    out = _mix_out(h, ma, sgb, outs, lses, w_branch_attn[0].astype(BF16), w_out[0].astype(BF16),
                   row(norm_ffn_g[0]), w_ffn_gate[0].astype(BF16), w_ffn_up[0].astype(BF16),
                   w_ffn_down[0].astype(BF16), row(norm_final_g))
    return out.reshape(batch, seq, d_model)
```

```python
import math

import jax
import jax.numpy as jnp
from jax import lax
from jax.experimental import pallas as pl
from jax.experimental.pallas import tpu as pltpu

F32 = jnp.float32
BF16 = jnp.bfloat16

D_MODEL = 1024
NORM_EPS = 1e-6
NEG_INF = -1e30
LOG2_E = math.log2(math.e)

CHUNK = 128
GMLP_WIDTH = D_MODEL
GMLP_GROUPS = 8

ATTN_PATTERNS = ((128, 1), (512, 4), (2048, 16))
DILATIONS = tuple(d for _, d in ATTN_PATTERNS)
N_GROUPS = len(ATTN_PATTERNS)
HEADS = 8
HEAD_DIM = 64
GROUP_WIDTH = HEADS * HEAD_DIM
QKV_WIDTH = N_GROUPS * GROUP_WIDTH
Q_BLOCK = 128
HALF = 64
KEY_WIN = Q_BLOCK + 2 * HALF
HEAD_PAIR = 2 * HEAD_DIM

OFF_U = 0
OFF_V = OFF_U + GMLP_WIDTH
OFF_Q = OFF_V + GMLP_WIDTH
OFF_K = OFF_Q + QKV_WIDTH
OFF_VA = OFF_K + QKV_WIDTH
OFF_GA = OFF_VA + QKV_WIDTH
OFF_GB = OFF_GA + D_MODEL
IN_WIDTH = OFF_GB + D_MODEL

LANES = 128
BF16_SUBLANES = 16
N_SLABS = GROUP_WIDTH // LANES
V7X_VMEM_BYTES = 64 * 1024 * 1024
MIX_IN_ROWS = 512
MIX_IN_SUBTILES = 2
MIX_OUT_ROWS = 256
ATTN_ROWS = 512


def _vmem_limit(nbytes):
    return int(min(nbytes * 3 // 2, V7X_VMEM_BYTES - (4 << 20)))


def _const_spec(shape):
    zeros = (0,) * len(shape)
    return pl.BlockSpec(shape, lambda *_: zeros, pipeline_mode=pl.Buffered(1))


def _rms_norm(x, g):
    return x * lax.rsqrt(jnp.mean(x * x, axis=-1, keepdims=True) + NORM_EPS) * g


def _dot(a, b):
    return jnp.dot(a, b, preferred_element_type=F32)


def _gelu(x):
    return 0.5 * x * (1.0 + lax.erf(x * math.sqrt(0.5)))


def _subsequence_major_copies(xn, slab_scr, slab4_scr, perm_scr):
    rows = xn.shape[0]
    n_slabs = xn.shape[1] // LANES
    step = DILATIONS[1]
    assert DILATIONS == (1, step, step * step)
    for s in range(n_slabs):
        slab_scr[s] = xn[:, s * LANES:(s + 1) * LANES]
    n1 = rows // step
    for r in range(step):
        for s in range(n_slabs):
            piece = slab_scr[s, pl.ds(r, n1, stride=step), :]
            perm_scr[0, r * n1:(r + 1) * n1, s * LANES:(s + 1) * LANES] = piece.astype(BF16)
            slab4_scr[r, s] = piece
    n2 = n1 // step
    for r2 in range(step * step):
        r, rho = r2 % step, r2 // step
        for s in range(n_slabs):
            piece = slab4_scr[r, s, pl.ds(rho, n2, stride=step), :]
            perm_scr[1, r2 * n2:(r2 + 1) * n2, s * LANES:(s + 1) * LANES] = piece.astype(BF16)


def _store_lane_blocks(val, out_ref, dilation):
    n = val.shape[0] // dilation
    for r in range(dilation):
        out_ref[:, r * GROUP_WIDTH:(r + 1) * GROUP_WIDTH] = val[r * n:(r + 1) * n, :].astype(out_ref.dtype)


def _mix_in_units(gmlp_first, x_ref, g_ref, win_ref, lng_ref, lnb_ref, ws_ref, bst_ref, wa_ref,
                  qkv_refs, ma_ref, sgb_ref, y_scr, slab_scr, slab4_scr, perm_scr):
    rows = x_ref.shape[0]
    xn_f32 = _rms_norm(x_ref[...], g_ref[...])
    xn = xn_f32.astype(BF16)

    def proj(off, width, lhs=None):
        return _dot(xn if lhs is None else lhs, win_ref[:, off:off + width])

    def projection_units():
        _subsequence_major_copies(xn_f32, slab_scr, slab4_scr, perm_scr)
        yield
        scale = LOG2_E / math.sqrt(HEAD_DIM)
        for gi in range(N_GROUPS):
            off = gi * GROUP_WIDTH
            lhs = xn if gi == 0 else perm_scr[gi - 1]
            q_ref, k_ref, v_ref = qkv_refs[gi], qkv_refs[N_GROUPS + gi], qkv_refs[2 * N_GROUPS + gi]
            _store_lane_blocks(proj(OFF_Q + off, GROUP_WIDTH, lhs) * scale, q_ref, DILATIONS[gi])
            yield
            _store_lane_blocks(proj(OFF_K + off, GROUP_WIDTH, lhs), k_ref, DILATIONS[gi])
            yield
            _store_lane_blocks(proj(OFF_VA + off, GROUP_WIDTH, lhs), v_ref, DILATIONS[gi])
            yield
        sgb_ref[...] = jax.nn.sigmoid(proj(OFF_GB, D_MODEL))
        yield

    def gmlp_units():
        gu = _gelu(proj(OFF_U, GMLP_WIDTH))
        yield
        gv = _gelu(proj(OFF_V, GMLP_WIDTH))
        mean = jnp.mean(gv, axis=-1, keepdims=True)
        cen = gv - mean
        var = jnp.mean(cen * cen, axis=-1, keepdims=True)
        vn = (cen * lax.rsqrt(var + NORM_EPS) * lng_ref[...] + lnb_ref[...]).astype(BF16)
        yield
        for c in range(rows // CHUNK):
            rs = slice(c * CHUNK, (c + 1) * CHUNK)
            for g in range(GMLP_GROUPS):
                cs = slice(g * CHUNK, (g + 1) * CHUNK)
                mixed = _dot(ws_ref[g], vn[rs, cs]) + bst_ref[:, g:g + 1]
                y_scr[rs, cs] = (gu[rs, cs] * mixed).astype(BF16)
            yield
        gate = jax.nn.sigmoid(proj(OFF_GA, D_MODEL))
        yield
        ma_ref[...] = gate * _dot(y_scr[...], wa_ref[...])
        yield

    for units in ((gmlp_units, projection_units) if gmlp_first else (projection_units, gmlp_units)):
        yield from units()


def _mix_in_kernel(x_ref, g_ref, win_ref, lng_ref, lnb_ref, ws_ref, bst_ref, wa_ref, *rest):
    qkv_refs, (ma_ref, sgb_ref, *scratch) = rest[:3 * N_GROUPS], rest[3 * N_GROUPS:]
    sub = x_ref.shape[0] // MIX_IN_SUBTILES
    bodies = []
    for t in range(MIX_IN_SUBTILES):
        rows_of = lambda ref, div=1: ref.at[pl.ds(t * sub // div, sub // div)]
        bodies.append(_mix_in_units(
            t % 2 == 1, rows_of(x_ref), g_ref, win_ref, lng_ref, lnb_ref, ws_ref, bst_ref, wa_ref,
            [rows_of(ref, DILATIONS[i % N_GROUPS]) for i, ref in enumerate(qkv_refs)],
            rows_of(ma_ref), rows_of(sgb_ref), *(scr.at[t] for scr in scratch)))
    while bodies:
        bodies = [b for b in bodies if next(b, "end") != "end"]


def _mix_in(x2d, norm_g, w_in, ln_g, ln_b, ws, bs_t, w_a):
    tokens = x2d.shape[0]
    rows = MIX_IN_ROWS
    sub = rows // MIX_IN_SUBTILES
    assert tokens % rows == 0 and sub % CHUNK == 0
    row_spec = lambda width: pl.BlockSpec((rows, width), lambda i: (i, 0))
    assert all(sub % (BF16_SUBLANES * d) == 0 for d in DILATIONS)
    step = DILATIONS[1]
    qkv_specs = [pl.BlockSpec((rows // d, d * GROUP_WIDTH), lambda i: (i, 0)) for d in DILATIONS] * 3
    qkv_shapes = [jax.ShapeDtypeStruct((tokens // d, d * GROUP_WIDTH), BF16) for d in DILATIONS] * 3
    f32_shape = jax.ShapeDtypeStruct((tokens, D_MODEL), F32)
    weight_bytes = 2 * (w_in.size + w_a.size + ws.size)
    tile_bytes = 2 * rows * (4 * D_MODEL + 9 * 2 * GROUP_WIDTH + 2 * 4 * D_MODEL)
    temp_bytes = rows * D_MODEL * (2 + 2 * 4 + 2 * 2 + 3 * 4)
    return pl.pallas_call(
        _mix_in_kernel,
        grid=(tokens // rows,),
        in_specs=[
            row_spec(D_MODEL),
            _const_spec((1, D_MODEL)),
            _const_spec((D_MODEL, IN_WIDTH)),
            _const_spec((1, GMLP_WIDTH)),
            _const_spec((1, GMLP_WIDTH)),
            _const_spec((GMLP_GROUPS, CHUNK, CHUNK)),
            _const_spec((CHUNK, GMLP_GROUPS)),
            _const_spec((GMLP_WIDTH, D_MODEL)),
        ],
        out_specs=qkv_specs + [row_spec(D_MODEL)] * 2,
        out_shape=qkv_shapes + [f32_shape] * 2,
        scratch_shapes=[
            pltpu.VMEM((MIX_IN_SUBTILES, sub, GMLP_WIDTH), BF16),
            pltpu.VMEM((MIX_IN_SUBTILES, D_MODEL // LANES, sub, LANES), F32),
            pltpu.VMEM((MIX_IN_SUBTILES, step, D_MODEL // LANES, sub // step, LANES), F32),
            pltpu.VMEM((MIX_IN_SUBTILES, N_GROUPS - 1, sub, D_MODEL), BF16),
        ],
        compiler_params=pltpu.CompilerParams(
            dimension_semantics=("parallel",),
            vmem_limit_bytes=_vmem_limit(weight_bytes + tile_bytes + temp_bytes)),
        name="mix_in",
    )(x2d, norm_g, w_in, ln_g, ln_b, ws, bs_t, w_a)


def _attn_kernel(q_ref, k_ref, kl_ref, kr_ref, v_ref, vl_ref, vr_ref, bias_ref,
                 o_ref, lse_ref):
    rows = q_ref.shape[0]
    n_blk = rows // Q_BLOCK
    first = pl.program_id(2) == 0
    last = pl.program_id(2) == pl.num_programs(2) - 1
    lane = lax.broadcasted_iota(jnp.int32, (Q_BLOCK, HEAD_PAIR), 1)
    low_half = lane < HEAD_DIM
    ones_cols = jnp.ones((KEY_WIN, LANES), BF16)

    for j in range(n_blk):
        lo = j * Q_BLOCK - HALF
        if j == 0:
            edge = jnp.where(first, 1, 0)
        elif j == n_blk - 1:
            edge = jnp.where(last, 2, 0)
        else:
            edge = 0
        for p in range(HEADS // 2):
            cs = slice(p * HEAD_PAIR, (p + 1) * HEAD_PAIR)

            def window(main_ref, left_ref, right_ref):
                if j == 0:
                    return jnp.concatenate([left_ref[:, cs], main_ref[0:lo + KEY_WIN, cs]], axis=0)
                if j == n_blk - 1:
                    return jnp.concatenate([main_ref[lo:rows, cs], right_ref[:, cs]], axis=0)
                return main_ref[lo:lo + KEY_WIN, cs]

            k_win = window(k_ref, kl_ref, kr_ref)
            v_aug = jnp.concatenate([window(v_ref, vl_ref, vr_ref), ones_cols], axis=1)
            q_pair = q_ref[j * Q_BLOCK:(j + 1) * Q_BLOCK, cs]
            accs, maxes = [], []
            for e in range(2):
                own_lanes = low_half if e == 0 else jnp.logical_not(low_half)
                q_head = jnp.where(own_lanes, q_pair, jnp.zeros_like(q_pair))
                s = lax.dot_general(q_head, k_win, (((1,), (1,)), ((), ())),
                                    preferred_element_type=F32)
                s = s + bias_ref[edge, 2 * p + e]
                m = jnp.max(s, axis=-1, keepdims=True)
                prob = jnp.exp2(s - m).astype(BF16)
                accs.append(_dot(prob, v_aug))
                maxes.append(m)
            numer = jnp.where(low_half, accs[0][:, :LANES], accs[1][:, :LANES])
            denom = jnp.where(low_half, accs[0][:, LANES:], accs[1][:, LANES:])
            row_max = jnp.where(low_half, maxes[0], maxes[1])
            rs = slice(j * Q_BLOCK, (j + 1) * Q_BLOCK)
            o_ref[rs, cs] = (numer / denom).astype(o_ref.dtype)
            lse_ref[rs, cs] = row_max + jnp.log2(denom)


def _attn_bias(group, dilation):
    n = N_GROUPS * HEADS
    idx = jnp.arange(1, n + 1, dtype=F32)
    slopes = jnp.exp2(-8.0 * idx / n).reshape(N_GROUPS, HEADS)[group]
    qi = jnp.arange(Q_BLOCK)[:, None]
    kj = jnp.arange(KEY_WIN)[None, :]
    rel = jnp.abs(kj - HALF - qi)
    dist = (rel * dilation).astype(F32)
    band = rel <= HALF
    in_seq = jnp.stack([jnp.ones_like(kj, dtype=bool), kj >= HALF, kj < KEY_WIN - HALF])
    valid = band[None] & in_seq
    score_bias = -LOG2_E * slopes[:, None, None] * dist[None]
    return jnp.where(valid[:, None], score_bias[None], NEG_INF)


def _attention_group(group, dilation, q, k, v, batch, seq):
    length = seq // dilation
    rows = min(ATTN_ROWS, length)
    assert length % rows == 0 and rows % Q_BLOCK == 0 and rows >= 2 * Q_BLOCK
    n_tiles = length // rows
    halo_per_tile = rows // HALF
    n_halo = length // HALF
    width = dilation * GROUP_WIDTH
    view = lambda t: t.reshape(batch, length, width)

    main = pl.BlockSpec((None, rows, GROUP_WIDTH), lambda b, r, t: (b, t, r))
    left = pl.BlockSpec((None, HALF, GROUP_WIDTH),
                        lambda b, r, t: (b, jnp.maximum(t * halo_per_tile - 1, 0), r))
    right = pl.BlockSpec((None, HALF, GROUP_WIDTH),
                         lambda b, r, t: (b, jnp.minimum((t + 1) * halo_per_tile, n_halo - 1), r))
    bias = _attn_bias(group, dilation)
    tile_bytes = 2 * (3 * rows * GROUP_WIDTH * 2 + 4 * HALF * GROUP_WIDTH * 2
                      + rows * GROUP_WIDTH * (2 + 4))
    out, lse = pl.pallas_call(
        _attn_kernel,
        grid=(batch, dilation, n_tiles),
        in_specs=[main, main, left, right, main, left, right,
                  _const_spec(bias.shape)],
        out_specs=[main, main],
        out_shape=[jax.ShapeDtypeStruct((batch, length, width), BF16),
                   jax.ShapeDtypeStruct((batch, length, width), F32)],
        compiler_params=pltpu.CompilerParams(
            dimension_semantics=("parallel", "parallel", "parallel"),
            vmem_limit_bytes=_vmem_limit(tile_bytes + bias.size * 4 + (8 << 20))),
        name=f"attn_dil{dilation}",
    )(view(q), view(k), view(k), view(k), view(v), view(v), view(v), bias)
    return out.reshape(batch * length, width), lse.reshape(batch * length, width)


def _load_token_major(ref, dilation, slab_scr):
    if dilation == 1:
        return ref[...].astype(F32)
    n = ref.shape[0]
    for r in range(dilation):
        for s in range(N_SLABS):
            lo = r * GROUP_WIDTH + s * LANES
            slab_scr[s, pl.ds(r, n, stride=dilation), :] = ref[:, lo:lo + LANES].astype(F32)
    return jnp.concatenate([slab_scr[s] for s in range(N_SLABS)], axis=1)


def _mix_out_kernel(x_ref, ma_ref, sgb_ref, o0_ref, o1_ref, o2_ref, l0_ref, l1_ref, l2_ref,
                    wb_ref, wo_ref, gf_ref, wg_ref, wu_ref, wd_ref, gl_ref, out_ref, slab_scr):
    l0, l1, l2 = (_load_token_major(ref, d, slab_scr.at[i])
                  for i, (ref, d) in enumerate(zip((l0_ref, l1_ref, l2_ref), DILATIONS)))
    o0, o1, o2 = (_load_token_major(ref, d, slab_scr.at[N_GROUPS + i])
                  for i, (ref, d) in enumerate(zip((o0_ref, o1_ref, o2_ref), DILATIONS)))
    top = jnp.maximum(jnp.maximum(l0, l1), l2)
    e0, e1, e2 = jnp.exp2(l0 - top), jnp.exp2(l1 - top), jnp.exp2(l2 - top)
    y = (e0 * o0 + e1 * o1 + e2 * o2) / (e0 + e1 + e2)
    merged = ma_ref[...] + sgb_ref[...] * _dot(y.astype(BF16), wb_ref[...])
    h = x_ref[...] + _dot(merged.astype(BF16), wo_ref[...])
    hn = _rms_norm(h, gf_ref[...]).astype(BF16)
    act = jax.nn.silu(_dot(hn, wg_ref[...])) * _dot(hn, wu_ref[...])
    h = h + _dot(act.astype(BF16), wd_ref[...])
    out_ref[...] = _rms_norm(h, gl_ref[...])


def _mix_out(x2d, ma, sgb, outs, lses, w_b, w_out, g_ffn, w_gate, w_up, w_down, g_final):
    tokens = x2d.shape[0]
    rows = MIX_OUT_ROWS
    assert tokens % rows == 0
    d_ff = w_gate.shape[1]
    row_spec = lambda width: pl.BlockSpec((rows, width), lambda i: (i, 0))
    assert all(rows % (BF16_SUBLANES * d) == 0 for d in DILATIONS)
    group_specs = [pl.BlockSpec((rows // d, d * GROUP_WIDTH), lambda i: (i, 0)) for d in DILATIONS]
    weights = (w_b, w_out, w_gate, w_up, w_down)
    weight_bytes = 2 * sum(w.size for w in weights)
    slab_shape = (2 * N_GROUPS, N_SLABS, rows, LANES)
    tile_bytes = 2 * rows * (4 * 4 * D_MODEL + 3 * GROUP_WIDTH * (2 + 4))
    temp_bytes = rows * (3 * 4 * d_ff + 4 * 4 * D_MODEL) + 4 * math.prod(slab_shape)
    return pl.pallas_call(
        _mix_out_kernel,
        grid=(tokens // rows,),
        scratch_shapes=[pltpu.VMEM(slab_shape, F32)],
        in_specs=[row_spec(D_MODEL)] * 3 + group_specs * 2 + [
            _const_spec(w_b.shape), _const_spec(w_out.shape), _const_spec((1, D_MODEL)),
            _const_spec(w_gate.shape), _const_spec(w_up.shape), _const_spec(w_down.shape),
            _const_spec((1, D_MODEL))],
        out_specs=row_spec(D_MODEL),
        out_shape=jax.ShapeDtypeStruct((tokens, D_MODEL), F32),
        compiler_params=pltpu.CompilerParams(
            dimension_semantics=("parallel",),
            vmem_limit_bytes=_vmem_limit(weight_bytes + tile_bytes + temp_bytes)),
        name="mix_out",
    )(x2d, ma, sgb, *outs, *lses, w_b, w_out, g_ffn, w_gate, w_up, w_down, g_final)


def kernel(x, norm_mix_g, w_in, gmlp_ln_g, gmlp_ln_b, gmlp_ws, gmlp_bs, w_branch_gmlp,
           w_branch_attn, w_out, norm_ffn_g, w_ffn_gate, w_ffn_up, w_ffn_down, norm_final_g):
    batch, seq, d_model = x.shape
    assert w_in.shape[0] == 1 and d_model == D_MODEL and w_in.shape[2] == IN_WIDTH
    for window, dilation in ATTN_PATTERNS:
        assert window == 2 * HALF * dilation
    tokens = batch * seq
    h = x.reshape(tokens, d_model)
    row = lambda v: v.reshape(1, -1).astype(F32)
    stage1 = _mix_in(
        h, row(norm_mix_g[0]), w_in[0].astype(BF16), row(gmlp_ln_g[0]), row(gmlp_ln_b[0]),
        gmlp_ws[0].astype(BF16), gmlp_bs[0].astype(F32).T, w_branch_gmlp[0].astype(BF16))
    qs, ks, vs = stage1[0:3], stage1[3:6], stage1[6:9]
    ma, sgb = stage1[9], stage1[10]
    outs, lses = [], []
    for gi, dilation in enumerate(DILATIONS):
        o, lse = _attention_group(gi, dilation, qs[gi], ks[gi], vs[gi], batch, seq)
        outs.append(o)
        lses.append(lse)
    out = _mix_out(h, ma, sgb, outs, lses, w_branch_attn[0].astype(BF16), w_out[0].astype(BF16),
                   row(norm_ffn_g[0]), w_ffn_gate[0].astype(BF16), w_ffn_up[0].astype(BF16),
                   w_ffn_down[0].astype(BF16), row(norm_final_g))
    return out.reshape(batch, seq, d_model)
```

```python
import math

import jax
import jax.numpy as jnp
from jax import lax
from jax.experimental import pallas as pl
from jax.experimental.pallas import tpu as pltpu

F32 = jnp.float32
BF16 = jnp.bfloat16

D_MODEL = 1024
NORM_EPS = 1e-6
NEG_INF = -1e30
LOG2_E = math.log2(math.e)

CHUNK = 128
GMLP_WIDTH = D_MODEL
GMLP_GROUPS = 8

ATTN_PATTERNS = ((128, 1), (512, 4), (2048, 16))
DILATIONS = tuple(d for _, d in ATTN_PATTERNS)
N_GROUPS = len(ATTN_PATTERNS)
HEADS = 8
HEAD_DIM = 64
GROUP_WIDTH = HEADS * HEAD_DIM
QKV_WIDTH = N_GROUPS * GROUP_WIDTH
Q_BLOCK = 128
HALF = 64
KEY_WIN = Q_BLOCK + 2 * HALF
HEAD_PAIR = 2 * HEAD_DIM

OFF_U = 0
OFF_V = OFF_U + GMLP_WIDTH
OFF_Q = OFF_V + GMLP_WIDTH
OFF_K = OFF_Q + QKV_WIDTH
OFF_VA = OFF_K + QKV_WIDTH
OFF_GA = OFF_VA + QKV_WIDTH
OFF_GB = OFF_GA + D_MODEL
IN_WIDTH = OFF_GB + D_MODEL

LANES = 128
BF16_SUBLANES = 16
N_SLABS = GROUP_WIDTH // LANES
V7X_VMEM_BYTES = 64 * 1024 * 1024
MIX_IN_ROWS = 512
MIX_IN_SUBTILES = 2
MIX_OUT_ROWS = 256
ATTN_ROWS = 2048


def _vmem_limit(nbytes):
    return int(min(nbytes * 3 // 2, V7X_VMEM_BYTES - (4 << 20)))


def _const_spec(shape):
    zeros = (0,) * len(shape)
    return pl.BlockSpec(shape, lambda *_: zeros, pipeline_mode=pl.Buffered(1))


def _rms_norm(x, g):
    return x * lax.rsqrt(jnp.mean(x * x, axis=-1, keepdims=True) + NORM_EPS) * g


def _dot(a, b):
    return jnp.dot(a, b, preferred_element_type=F32)


def _gelu(x):
    return 0.5 * x * (1.0 + lax.erf(x * math.sqrt(0.5)))


def _subsequence_major_copies(xn, slab_scr, slab4_scr, perm_scr):
    rows = xn.shape[0]
    n_slabs = xn.shape[1] // LANES
    step = DILATIONS[1]
    assert DILATIONS == (1, step, step * step)
    for s in range(n_slabs):
        slab_scr[s] = xn[:, s * LANES:(s + 1) * LANES]
    n1 = rows // step
    for r in range(step):
        for s in range(n_slabs):
            piece = slab_scr[s, pl.ds(r, n1, stride=step), :]
            perm_scr[0, r * n1:(r + 1) * n1, s * LANES:(s + 1) * LANES] = piece.astype(BF16)
            slab4_scr[r, s] = piece
    n2 = n1 // step
    for r2 in range(step * step):
        r, rho = r2 % step, r2 // step
        for s in range(n_slabs):
            piece = slab4_scr[r, s, pl.ds(rho, n2, stride=step), :]
            perm_scr[1, r2 * n2:(r2 + 1) * n2, s * LANES:(s + 1) * LANES] = piece.astype(BF16)


def _store_lane_blocks(val, out_ref, dilation):
    n = val.shape[0] // dilation
    for r in range(dilation):
        out_ref[:, r * GROUP_WIDTH:(r + 1) * GROUP_WIDTH] = val[r * n:(r + 1) * n, :].astype(out_ref.dtype)


def _mix_in_units(gmlp_first, x_ref, g_ref, win_ref, lng_ref, lnb_ref, ws_ref, bst_ref, wa_ref,
                  qkv_refs, ma_ref, sgb_ref, y_scr, slab_scr, slab4_scr, perm_scr):
    rows = x_ref.shape[0]
    xn_f32 = _rms_norm(x_ref[...], g_ref[...])
    xn = xn_f32.astype(BF16)

    def proj(off, width, lhs=None):
        return _dot(xn if lhs is None else lhs, win_ref[:, off:off + width])

    def projection_units():
        _subsequence_major_copies(xn_f32, slab_scr, slab4_scr, perm_scr)
        yield
        scale = LOG2_E / math.sqrt(HEAD_DIM)
        for gi in range(N_GROUPS):
            off = gi * GROUP_WIDTH
            lhs = xn if gi == 0 else perm_scr[gi - 1]
            q_ref, k_ref, v_ref = qkv_refs[gi], qkv_refs[N_GROUPS + gi], qkv_refs[2 * N_GROUPS + gi]
            _store_lane_blocks(proj(OFF_Q + off, GROUP_WIDTH, lhs) * scale, q_ref, DILATIONS[gi])
            yield
            _store_lane_blocks(proj(OFF_K + off, GROUP_WIDTH, lhs), k_ref, DILATIONS[gi])
            yield
            _store_lane_blocks(proj(OFF_VA + off, GROUP_WIDTH, lhs), v_ref, DILATIONS[gi])
            yield
        sgb_ref[...] = jax.nn.sigmoid(proj(OFF_GB, D_MODEL))
        yield

    def gmlp_units():
        gu = _gelu(proj(OFF_U, GMLP_WIDTH))
        yield
        gv = _gelu(proj(OFF_V, GMLP_WIDTH))
        mean = jnp.mean(gv, axis=-1, keepdims=True)
        cen = gv - mean
        var = jnp.mean(cen * cen, axis=-1, keepdims=True)
        vn = (cen * lax.rsqrt(var + NORM_EPS) * lng_ref[...] + lnb_ref[...]).astype(BF16)
        yield
        for c in range(rows // CHUNK):
            rs = slice(c * CHUNK, (c + 1) * CHUNK)
            for g in range(GMLP_GROUPS):
                cs = slice(g * CHUNK, (g + 1) * CHUNK)
                mixed = _dot(ws_ref[g], vn[rs, cs]) + bst_ref[:, g:g + 1]
                y_scr[rs, cs] = (gu[rs, cs] * mixed).astype(BF16)
            yield
        gate = jax.nn.sigmoid(proj(OFF_GA, D_MODEL))
        yield
        ma_ref[...] = gate * _dot(y_scr[...], wa_ref[...])
        yield

    for units in ((gmlp_units, projection_units) if gmlp_first else (projection_units, gmlp_units)):
        yield from units()


def _mix_in_kernel(x_ref, g_ref, win_ref, lng_ref, lnb_ref, ws_ref, bst_ref, wa_ref, *rest):
    qkv_refs, (ma_ref, sgb_ref, *scratch) = rest[:3 * N_GROUPS], rest[3 * N_GROUPS:]
    sub = x_ref.shape[0] // MIX_IN_SUBTILES
    bodies = []
    for t in range(MIX_IN_SUBTILES):
        rows_of = lambda ref, div=1: ref.at[pl.ds(t * sub // div, sub // div)]
        bodies.append(_mix_in_units(
            t % 2 == 1, rows_of(x_ref), g_ref, win_ref, lng_ref, lnb_ref, ws_ref, bst_ref, wa_ref,
            [rows_of(ref, DILATIONS[i % N_GROUPS]) for i, ref in enumerate(qkv_refs)],
            rows_of(ma_ref), rows_of(sgb_ref), *(scr.at[t] for scr in scratch)))
    while bodies:
        bodies = [b for b in bodies if next(b, "end") != "end"]


def _mix_in(x2d, norm_g, w_in, ln_g, ln_b, ws, bs_t, w_a):
    tokens = x2d.shape[0]
    rows = MIX_IN_ROWS
    sub = rows // MIX_IN_SUBTILES
    assert tokens % rows == 0 and sub % CHUNK == 0
    row_spec = lambda width: pl.BlockSpec((rows, width), lambda i: (i, 0))
    assert all(sub % (BF16_SUBLANES * d) == 0 for d in DILATIONS)
    step = DILATIONS[1]
    qkv_specs = [pl.BlockSpec((rows // d, d * GROUP_WIDTH), lambda i: (i, 0)) for d in DILATIONS] * 3
    qkv_shapes = [jax.ShapeDtypeStruct((tokens // d, d * GROUP_WIDTH), BF16) for d in DILATIONS] * 3
    f32_shape = jax.ShapeDtypeStruct((tokens, D_MODEL), F32)
    weight_bytes = 2 * (w_in.size + w_a.size + ws.size)
    tile_bytes = 2 * rows * (4 * D_MODEL + 9 * 2 * GROUP_WIDTH + 2 * 4 * D_MODEL)
    temp_bytes = rows * D_MODEL * (2 + 2 * 4 + 2 * 2 + 3 * 4)
    return pl.pallas_call(
        _mix_in_kernel,
        grid=(tokens // rows,),
        in_specs=[
            row_spec(D_MODEL),
            _const_spec((1, D_MODEL)),
            _const_spec((D_MODEL, IN_WIDTH)),
            _const_spec((1, GMLP_WIDTH)),
            _const_spec((1, GMLP_WIDTH)),
            _const_spec((GMLP_GROUPS, CHUNK, CHUNK)),
            _const_spec((CHUNK, GMLP_GROUPS)),
            _const_spec((GMLP_WIDTH, D_MODEL)),
        ],
        out_specs=qkv_specs + [row_spec(D_MODEL)] * 2,
        out_shape=qkv_shapes + [f32_shape] * 2,
        scratch_shapes=[
            pltpu.VMEM((MIX_IN_SUBTILES, sub, GMLP_WIDTH), BF16),
            pltpu.VMEM((MIX_IN_SUBTILES, D_MODEL // LANES, sub, LANES), F32),
            pltpu.VMEM((MIX_IN_SUBTILES, step, D_MODEL // LANES, sub // step, LANES), F32),
            pltpu.VMEM((MIX_IN_SUBTILES, N_GROUPS - 1, sub, D_MODEL), BF16),
        ],
        compiler_params=pltpu.CompilerParams(
            dimension_semantics=("parallel",),
            vmem_limit_bytes=_vmem_limit(weight_bytes + tile_bytes + temp_bytes)),
        name="mix_in",
    )(x2d, norm_g, w_in, ln_g, ln_b, ws, bs_t, w_a)


def _attn_kernel(q_ref, k_ref, kl_ref, kr_ref, v_ref, vl_ref, vr_ref, bias_ref,
                 o_ref, lse_ref):
    rows = q_ref.shape[0]
    n_blk = rows // Q_BLOCK
    first = pl.program_id(2) == 0
    last = pl.program_id(2) == pl.num_programs(2) - 1
    lane = lax.broadcasted_iota(jnp.int32, (Q_BLOCK, HEAD_PAIR), 1)
    low_half = lane < HEAD_DIM
    ones_cols = jnp.ones((KEY_WIN, LANES), BF16)

    n_sub = q_ref.shape[1] // GROUP_WIDTH
    for j, u in ((j, u) for u in range(n_sub) for j in range(n_blk)):
        lo = j * Q_BLOCK - HALF
        if j == 0:
            edge = jnp.where(first, 1, 0)
        elif j == n_blk - 1:
            edge = jnp.where(last, 2, 0)
        else:
            edge = 0
        for p in range(HEADS // 2):
            cs = slice(u * GROUP_WIDTH + p * HEAD_PAIR, u * GROUP_WIDTH + (p + 1) * HEAD_PAIR)

            def window(main_ref, left_ref, right_ref):
                if j == 0:
                    return jnp.concatenate([left_ref[:, cs], main_ref[0:lo + KEY_WIN, cs]], axis=0)
                if j == n_blk - 1:
                    return jnp.concatenate([main_ref[lo:rows, cs], right_ref[:, cs]], axis=0)
                return main_ref[lo:lo + KEY_WIN, cs]

            k_win = window(k_ref, kl_ref, kr_ref)
            v_aug = jnp.concatenate([window(v_ref, vl_ref, vr_ref), ones_cols], axis=1)
            q_pair = q_ref[j * Q_BLOCK:(j + 1) * Q_BLOCK, cs]
            accs, maxes = [], []
            for e in range(2):
                own_lanes = low_half if e == 0 else jnp.logical_not(low_half)
                q_head = jnp.where(own_lanes, q_pair, jnp.zeros_like(q_pair))
                s = lax.dot_general(q_head, k_win, (((1,), (1,)), ((), ())),
                                    preferred_element_type=F32)
                s = s + bias_ref[edge, 2 * p + e]
                m = jnp.max(s, axis=-1, keepdims=True)
                prob = jnp.exp2(s - m).astype(BF16)
                accs.append(_dot(prob, v_aug))
                maxes.append(m)
            numer = jnp.where(low_half, accs[0][:, :LANES], accs[1][:, :LANES])
            denom = jnp.where(low_half, accs[0][:, LANES:], accs[1][:, LANES:])
            row_max = jnp.where(low_half, maxes[0], maxes[1])
            rs = slice(j * Q_BLOCK, (j + 1) * Q_BLOCK)
            o_ref[rs, cs] = (numer / denom).astype(o_ref.dtype)
            lse_ref[rs, cs] = row_max + jnp.log2(denom)


def _attn_bias(group, dilation):
    n = N_GROUPS * HEADS
    idx = jnp.arange(1, n + 1, dtype=F32)
    slopes = jnp.exp2(-8.0 * idx / n).reshape(N_GROUPS, HEADS)[group]
    qi = jnp.arange(Q_BLOCK)[:, None]
    kj = jnp.arange(KEY_WIN)[None, :]
    rel = jnp.abs(kj - HALF - qi)
    dist = (rel * dilation).astype(F32)
    band = rel <= HALF
    in_seq = jnp.stack([jnp.ones_like(kj, dtype=bool), kj >= HALF, kj < KEY_WIN - HALF])
    valid = band[None] & in_seq
    score_bias = -LOG2_E * slopes[:, None, None] * dist[None]
    return jnp.where(valid[:, None], score_bias[None], NEG_INF)


def _attention_group(group, dilation, q, k, v, batch, seq):
    length = seq // dilation
    rows = min(ATTN_ROWS, length)
    n_sub = min(dilation, ATTN_ROWS // rows)
    assert length % rows == 0 and rows % Q_BLOCK == 0 and rows >= 2 * Q_BLOCK and dilation % n_sub == 0
    n_tiles = length // rows
    halo_per_tile = rows // HALF
    n_halo = length // HALF
    width = dilation * GROUP_WIDTH
    step_width = n_sub * GROUP_WIDTH
    view = lambda t: t.reshape(batch, length, width)

    main = pl.BlockSpec((None, rows, step_width), lambda b, r, t: (b, t, r))
    left = pl.BlockSpec((None, HALF, step_width),
                        lambda b, r, t: (b, jnp.maximum(t * halo_per_tile - 1, 0), r))
    right = pl.BlockSpec((None, HALF, step_width),
                         lambda b, r, t: (b, jnp.minimum((t + 1) * halo_per_tile, n_halo - 1), r))
    bias = _attn_bias(group, dilation)
    tile_bytes = 2 * (3 * rows * step_width * 2 + 4 * HALF * step_width * 2
                      + rows * step_width * (2 + 4))
    out, lse = pl.pallas_call(
        _attn_kernel,
        grid=(batch, dilation // n_sub, n_tiles),
        in_specs=[main, main, left, right, main, left, right,
                  _const_spec(bias.shape)],
        out_specs=[main, main],
        out_shape=[jax.ShapeDtypeStruct((batch, length, width), BF16),
                   jax.ShapeDtypeStruct((batch, length, width), F32)],
        compiler_params=pltpu.CompilerParams(
            dimension_semantics=("parallel", "parallel", "parallel"),
            vmem_limit_bytes=_vmem_limit(tile_bytes + bias.size * 4 + (8 << 20))),
        name=f"attn_dil{dilation}",
    )(view(q), view(k), view(k), view(k), view(v), view(v), view(v), bias)
    return out.reshape(batch * length, width), lse.reshape(batch * length, width)


def _load_token_major(ref, dilation, slab_scr):
    if dilation == 1:
        return ref[...].astype(F32)
    n = ref.shape[0]
    for r in range(dilation):
        for s in range(N_SLABS):
            lo = r * GROUP_WIDTH + s * LANES
            slab_scr[s, pl.ds(r, n, stride=dilation), :] = ref[:, lo:lo + LANES].astype(F32)
    return jnp.concatenate([slab_scr[s] for s in range(N_SLABS)], axis=1)


def _mix_out_kernel(x_ref, ma_ref, sgb_ref, o0_ref, o1_ref, o2_ref, l0_ref, l1_ref, l2_ref,
                    wb_ref, wo_ref, gf_ref, wg_ref, wu_ref, wd_ref, gl_ref, out_ref, slab_scr):
    l0, l1, l2 = (_load_token_major(ref, d, slab_scr.at[i])
                  for i, (ref, d) in enumerate(zip((l0_ref, l1_ref, l2_ref), DILATIONS)))
    o0, o1, o2 = (_load_token_major(ref, d, slab_scr.at[N_GROUPS + i])
                  for i, (ref, d) in enumerate(zip((o0_ref, o1_ref, o2_ref), DILATIONS)))
    top = jnp.maximum(jnp.maximum(l0, l1), l2)
    e0, e1, e2 = jnp.exp2(l0 - top), jnp.exp2(l1 - top), jnp.exp2(l2 - top)
    y = (e0 * o0 + e1 * o1 + e2 * o2) / (e0 + e1 + e2)
    merged = ma_ref[...] + sgb_ref[...] * _dot(y.astype(BF16), wb_ref[...])
    h = x_ref[...] + _dot(merged.astype(BF16), wo_ref[...])
    hn = _rms_norm(h, gf_ref[...]).astype(BF16)
    act = jax.nn.silu(_dot(hn, wg_ref[...])) * _dot(hn, wu_ref[...])
    h = h + _dot(act.astype(BF16), wd_ref[...])
    out_ref[...] = _rms_norm(h, gl_ref[...])


def _mix_out(x2d, ma, sgb, outs, lses, w_b, w_out, g_ffn, w_gate, w_up, w_down, g_final):
    tokens = x2d.shape[0]
    rows = MIX_OUT_ROWS
    assert tokens % rows == 0
    d_ff = w_gate.shape[1]
    row_spec = lambda width: pl.BlockSpec((rows, width), lambda i: (i, 0))
    assert all(rows % (BF16_SUBLANES * d) == 0 for d in DILATIONS)
    group_specs = [pl.BlockSpec((rows // d, d * GROUP_WIDTH), lambda i: (i, 0)) for d in DILATIONS]
    weights = (w_b, w_out, w_gate, w_up, w_down)
    weight_bytes = 2 * sum(w.size for w in weights)
    slab_shape = (2 * N_GROUPS, N_SLABS, rows, LANES)
    tile_bytes = 2 * rows * (4 * 4 * D_MODEL + 3 * GROUP_WIDTH * (2 + 4))
    temp_bytes = rows * (3 * 4 * d_ff + 4 * 4 * D_MODEL) + 4 * math.prod(slab_shape)
    return pl.pallas_call(
        _mix_out_kernel,
        grid=(tokens // rows,),
        scratch_shapes=[pltpu.VMEM(slab_shape, F32)],
        in_specs=[row_spec(D_MODEL)] * 3 + group_specs * 2 + [
            _const_spec(w_b.shape), _const_spec(w_out.shape), _const_spec((1, D_MODEL)),
            _const_spec(w_gate.shape), _const_spec(w_up.shape), _const_spec(w_down.shape),
            _const_spec((1, D_MODEL))],
        out_specs=row_spec(D_MODEL),
        out_shape=jax.ShapeDtypeStruct((tokens, D_MODEL), F32),
        compiler_params=pltpu.CompilerParams(
            dimension_semantics=("parallel",),
            vmem_limit_bytes=_vmem_limit(weight_bytes + tile_bytes + temp_bytes)),
        name="mix_out",
    )(x2d, ma, sgb, *outs, *lses, w_b, w_out, g_ffn, w_gate, w_up, w_down, g_final)


def kernel(x, norm_mix_g, w_in, gmlp_ln_g, gmlp_ln_b, gmlp_ws, gmlp_bs, w_branch_gmlp,
           w_branch_attn, w_out, norm_ffn_g, w_ffn_gate, w_ffn_up, w_ffn_down, norm_final_g):
    batch, seq, d_model = x.shape
    assert w_in.shape[0] == 1 and d_model == D_MODEL and w_in.shape[2] == IN_WIDTH
    for window, dilation in ATTN_PATTERNS:
        assert window == 2 * HALF * dilation
    tokens = batch * seq
    h = x.reshape(tokens, d_model)
    row = lambda v: v.reshape(1, -1).astype(F32)
    stage1 = _mix_in(
        h, row(norm_mix_g[0]), w_in[0].astype(BF16), row(gmlp_ln_g[0]), row(gmlp_ln_b[0]),
        gmlp_ws[0].astype(BF16), gmlp_bs[0].astype(F32).T, w_branch_gmlp[0].astype(BF16))
    qs, ks, vs = stage1[0:3], stage1[3:6], stage1[6:9]
    ma, sgb = stage1[9], stage1[10]
    outs, lses = [], []
    for gi, dilation in enumerate(DILATIONS):
        o, lse = _attention_group(gi, dilation, qs[gi], ks[gi], vs[gi], batch, seq)
        outs.append(o)
        lses.append(lse)
    out = _mix_out(h, ma, sgb, outs, lses, w_branch_attn[0].astype(BF16), w_out[0].astype(BF16),
                   row(norm_ffn_g[0]), w_ffn_gate[0].astype(BF16), w_ffn_up[0].astype(BF16),
                   w_ffn_down[0].astype(BF16), row(norm_final_g))
    return out.reshape(batch, seq, d_model)
```

```python
import math

import jax
import jax.numpy as jnp
from jax import lax
from jax.experimental import pallas as pl
from jax.experimental.pallas import tpu as pltpu

F32 = jnp.float32
BF16 = jnp.bfloat16

D_MODEL = 1024
NORM_EPS = 1e-6
NEG_INF = -1e30
LOG2_E = math.log2(math.e)

CHUNK = 128
GMLP_WIDTH = D_MODEL
GMLP_GROUPS = 8

ATTN_PATTERNS = ((128, 1), (512, 4), (2048, 16))
DILATIONS = tuple(d for _, d in ATTN_PATTERNS)
N_GROUPS = len(ATTN_PATTERNS)
HEADS = 8
HEAD_DIM = 64
GROUP_WIDTH = HEADS * HEAD_DIM
QKV_WIDTH = N_GROUPS * GROUP_WIDTH
Q_BLOCK = 128
HALF = 64
KEY_WIN = Q_BLOCK + 2 * HALF
HEAD_PAIR = 2 * HEAD_DIM

OFF_U = 0
OFF_V = OFF_U + GMLP_WIDTH
OFF_Q = OFF_V + GMLP_WIDTH
OFF_K = OFF_Q + QKV_WIDTH
OFF_VA = OFF_K + QKV_WIDTH
OFF_GA = OFF_VA + QKV_WIDTH
OFF_GB = OFF_GA + D_MODEL
IN_WIDTH = OFF_GB + D_MODEL

LANES = 128
BF16_SUBLANES = 16
N_SLABS = GROUP_WIDTH // LANES
V7X_VMEM_BYTES = 64 * 1024 * 1024
MIX_IN_ROWS = 512
MIX_IN_SUBTILES = 2
MIX_OUT_ROWS = 512
MIX_OUT_SUBTILES = 2
ATTN_ROWS = 2048


def _vmem_limit(nbytes):
    return int(min(nbytes * 3 // 2, V7X_VMEM_BYTES - (4 << 20)))


def _const_spec(shape):
    zeros = (0,) * len(shape)
    return pl.BlockSpec(shape, lambda *_: zeros, pipeline_mode=pl.Buffered(1))


def _rms_norm(x, g):
    return x * lax.rsqrt(jnp.mean(x * x, axis=-1, keepdims=True) + NORM_EPS) * g


def _dot(a, b):
    return jnp.dot(a, b, preferred_element_type=F32)


def _gelu(x):
    return 0.5 * x * (1.0 + lax.erf(x * math.sqrt(0.5)))


def _subsequence_major_copies(xn, slab_scr, slab4_scr, perm_scr):
    rows = xn.shape[0]
    n_slabs = xn.shape[1] // LANES
    step = DILATIONS[1]
    assert DILATIONS == (1, step, step * step)
    for s in range(n_slabs):
        slab_scr[s] = xn[:, s * LANES:(s + 1) * LANES]
    n1 = rows // step
    for r in range(step):
        for s in range(n_slabs):
            piece = slab_scr[s, pl.ds(r, n1, stride=step), :]
            perm_scr[0, r * n1:(r + 1) * n1, s * LANES:(s + 1) * LANES] = piece.astype(BF16)
            slab4_scr[r, s] = piece
    n2 = n1 // step
    for r2 in range(step * step):
        r, rho = r2 % step, r2 // step
        for s in range(n_slabs):
            piece = slab4_scr[r, s, pl.ds(rho, n2, stride=step), :]
            perm_scr[1, r2 * n2:(r2 + 1) * n2, s * LANES:(s + 1) * LANES] = piece.astype(BF16)


def _store_lane_blocks(val, out_ref, dilation):
    n = val.shape[0] // dilation
    for r in range(dilation):
        out_ref[:, r * GROUP_WIDTH:(r + 1) * GROUP_WIDTH] = val[r * n:(r + 1) * n, :].astype(out_ref.dtype)


def _mix_in_units(gmlp_first, x_ref, g_ref, win_ref, lng_ref, lnb_ref, ws_ref, bst_ref, wa_ref,
                  qkv_refs, ma_ref, sgb_ref, y_scr, slab_scr, slab4_scr, perm_scr):
    rows = x_ref.shape[0]
    xn_f32 = _rms_norm(x_ref[...], g_ref[...])
    xn = xn_f32.astype(BF16)

    def proj(off, width, lhs=None):
        return _dot(xn if lhs is None else lhs, win_ref[:, off:off + width])

    def projection_units():
        _subsequence_major_copies(xn_f32, slab_scr, slab4_scr, perm_scr)
        yield
        scale = LOG2_E / math.sqrt(HEAD_DIM)
        for gi in range(N_GROUPS):
            off = gi * GROUP_WIDTH
            lhs = xn if gi == 0 else perm_scr[gi - 1]
            q_ref, k_ref, v_ref = qkv_refs[gi], qkv_refs[N_GROUPS + gi], qkv_refs[2 * N_GROUPS + gi]
            _store_lane_blocks(proj(OFF_Q + off, GROUP_WIDTH, lhs) * scale, q_ref, DILATIONS[gi])
            yield
            _store_lane_blocks(proj(OFF_K + off, GROUP_WIDTH, lhs), k_ref, DILATIONS[gi])
            yield
            _store_lane_blocks(proj(OFF_VA + off, GROUP_WIDTH, lhs), v_ref, DILATIONS[gi])
            yield
        sgb_ref[...] = jax.nn.sigmoid(proj(OFF_GB, D_MODEL))
        yield

    def gmlp_units():
        gu = _gelu(proj(OFF_U, GMLP_WIDTH))
        yield
        gv = _gelu(proj(OFF_V, GMLP_WIDTH))
        mean = jnp.mean(gv, axis=-1, keepdims=True)
        cen = gv - mean
        var = jnp.mean(cen * cen, axis=-1, keepdims=True)
        vn = (cen * lax.rsqrt(var + NORM_EPS) * lng_ref[...] + lnb_ref[...]).astype(BF16)
        yield
        for c in range(rows // CHUNK):
            rs = slice(c * CHUNK, (c + 1) * CHUNK)
            for g in range(GMLP_GROUPS):
                cs = slice(g * CHUNK, (g + 1) * CHUNK)
                mixed = _dot(ws_ref[g], vn[rs, cs]) + bst_ref[:, g:g + 1]
                y_scr[rs, cs] = (gu[rs, cs] * mixed).astype(BF16)
            yield
        gate = jax.nn.sigmoid(proj(OFF_GA, D_MODEL))
        yield
        ma_ref[...] = gate * _dot(y_scr[...], wa_ref[...])
        yield

    for units in ((gmlp_units, projection_units) if gmlp_first else (projection_units, gmlp_units)):
        yield from units()


def _mix_in_kernel(x_ref, g_ref, win_ref, lng_ref, lnb_ref, ws_ref, bst_ref, wa_ref, *rest):
    qkv_refs, (ma_ref, sgb_ref, *scratch) = rest[:3 * N_GROUPS], rest[3 * N_GROUPS:]
    sub = x_ref.shape[0] // MIX_IN_SUBTILES
    bodies = []
    for t in range(MIX_IN_SUBTILES):
        rows_of = lambda ref, div=1: ref.at[pl.ds(t * sub // div, sub // div)]
        bodies.append(_mix_in_units(
            t % 2 == 1, rows_of(x_ref), g_ref, win_ref, lng_ref, lnb_ref, ws_ref, bst_ref, wa_ref,
            [rows_of(ref, DILATIONS[i % N_GROUPS]) for i, ref in enumerate(qkv_refs)],
            rows_of(ma_ref), rows_of(sgb_ref), *(scr.at[t] for scr in scratch)))
    while bodies:
        bodies = [b for b in bodies if next(b, "end") != "end"]


def _mix_in(x2d, norm_g, w_in, ln_g, ln_b, ws, bs_t, w_a):
    tokens = x2d.shape[0]
    rows = MIX_IN_ROWS
    sub = rows // MIX_IN_SUBTILES
    assert tokens % rows == 0 and sub % CHUNK == 0
    row_spec = lambda width: pl.BlockSpec((rows, width), lambda i: (i, 0))
    assert all(sub % (BF16_SUBLANES * d) == 0 for d in DILATIONS)
    step = DILATIONS[1]
    qkv_specs = [pl.BlockSpec((rows // d, d * GROUP_WIDTH), lambda i: (i, 0)) for d in DILATIONS] * 3
    qkv_shapes = [jax.ShapeDtypeStruct((tokens // d, d * GROUP_WIDTH), BF16) for d in DILATIONS] * 3
    f32_shape = jax.ShapeDtypeStruct((tokens, D_MODEL), F32)
    weight_bytes = 2 * (w_in.size + w_a.size + ws.size)
    tile_bytes = 2 * rows * (4 * D_MODEL + 9 * 2 * GROUP_WIDTH + 2 * 4 * D_MODEL)
    temp_bytes = rows * D_MODEL * (2 + 2 * 4 + 2 * 2 + 3 * 4)
    return pl.pallas_call(
        _mix_in_kernel,
        grid=(tokens // rows,),
        in_specs=[
            row_spec(D_MODEL),
            _const_spec((1, D_MODEL)),
            _const_spec((D_MODEL, IN_WIDTH)),
            _const_spec((1, GMLP_WIDTH)),
            _const_spec((1, GMLP_WIDTH)),
            _const_spec((GMLP_GROUPS, CHUNK, CHUNK)),
            _const_spec((CHUNK, GMLP_GROUPS)),
            _const_spec((GMLP_WIDTH, D_MODEL)),
        ],
        out_specs=qkv_specs + [row_spec(D_MODEL)] * 2,
        out_shape=qkv_shapes + [f32_shape] * 2,
        scratch_shapes=[
            pltpu.VMEM((MIX_IN_SUBTILES, sub, GMLP_WIDTH), BF16),
            pltpu.VMEM((MIX_IN_SUBTILES, D_MODEL // LANES, sub, LANES), F32),
            pltpu.VMEM((MIX_IN_SUBTILES, step, D_MODEL // LANES, sub // step, LANES), F32),
            pltpu.VMEM((MIX_IN_SUBTILES, N_GROUPS - 1, sub, D_MODEL), BF16),
        ],
        compiler_params=pltpu.CompilerParams(
            dimension_semantics=("parallel",),
            vmem_limit_bytes=_vmem_limit(weight_bytes + tile_bytes + temp_bytes)),
        name="mix_in",
    )(x2d, norm_g, w_in, ln_g, ln_b, ws, bs_t, w_a)


def _attn_kernel(q_ref, k_ref, kl_ref, kr_ref, v_ref, vl_ref, vr_ref, bias_ref,
                 o_ref, lse_ref):
    rows = q_ref.shape[0]
    n_blk = rows // Q_BLOCK
    first = pl.program_id(2) == 0
    last = pl.program_id(2) == pl.num_programs(2) - 1
    lane = lax.broadcasted_iota(jnp.int32, (Q_BLOCK, HEAD_PAIR), 1)
    low_half = lane < HEAD_DIM
    ones_cols = jnp.ones((KEY_WIN, LANES), BF16)

    n_sub = q_ref.shape[1] // GROUP_WIDTH
    for j, u in ((j, u) for u in range(n_sub) for j in range(n_blk)):
        lo = j * Q_BLOCK - HALF
        if j == 0:
            edge = jnp.where(first, 1, 0)
        elif j == n_blk - 1:
            edge = jnp.where(last, 2, 0)
        else:
            edge = 0
        for p in range(HEADS // 2):
            cs = slice(u * GROUP_WIDTH + p * HEAD_PAIR, u * GROUP_WIDTH + (p + 1) * HEAD_PAIR)

            def window(main_ref, left_ref, right_ref):
                if j == 0:
                    return jnp.concatenate([left_ref[:, cs], main_ref[0:lo + KEY_WIN, cs]], axis=0)
                if j == n_blk - 1:
                    return jnp.concatenate([main_ref[lo:rows, cs], right_ref[:, cs]], axis=0)
                return main_ref[lo:lo + KEY_WIN, cs]

            k_win = window(k_ref, kl_ref, kr_ref)
            v_aug = jnp.concatenate([window(v_ref, vl_ref, vr_ref), ones_cols], axis=1)
            q_pair = q_ref[j * Q_BLOCK:(j + 1) * Q_BLOCK, cs]
            accs, maxes = [], []
            for e in range(2):
                own_lanes = low_half if e == 0 else jnp.logical_not(low_half)
                q_head = jnp.where(own_lanes, q_pair, jnp.zeros_like(q_pair))
                s = lax.dot_general(q_head, k_win, (((1,), (1,)), ((), ())),
                                    preferred_element_type=F32)
                s = s + bias_ref[edge, 2 * p + e]
                m = jnp.max(s, axis=-1, keepdims=True)
                prob = jnp.exp2(s - m).astype(BF16)
                accs.append(_dot(prob, v_aug))
                maxes.append(m)
            numer = jnp.where(low_half, accs[0][:, :LANES], accs[1][:, :LANES])
            denom = jnp.where(low_half, accs[0][:, LANES:], accs[1][:, LANES:])
            row_max = jnp.where(low_half, maxes[0], maxes[1])
            rs = slice(j * Q_BLOCK, (j + 1) * Q_BLOCK)
            o_ref[rs, cs] = (numer / denom).astype(o_ref.dtype)
            lse_ref[rs, cs] = row_max + jnp.log2(denom)


def _attn_bias(group, dilation):
    n = N_GROUPS * HEADS
    idx = jnp.arange(1, n + 1, dtype=F32)
    slopes = jnp.exp2(-8.0 * idx / n).reshape(N_GROUPS, HEADS)[group]
    qi = jnp.arange(Q_BLOCK)[:, None]
    kj = jnp.arange(KEY_WIN)[None, :]
    rel = jnp.abs(kj - HALF - qi)
    dist = (rel * dilation).astype(F32)
    band = rel <= HALF
    in_seq = jnp.stack([jnp.ones_like(kj, dtype=bool), kj >= HALF, kj < KEY_WIN - HALF])
    valid = band[None] & in_seq
    score_bias = -LOG2_E * slopes[:, None, None] * dist[None]
    return jnp.where(valid[:, None], score_bias[None], NEG_INF)


def _attention_group(group, dilation, q, k, v, batch, seq):
    length = seq // dilation
    rows = min(ATTN_ROWS, length)
    n_sub = min(dilation, ATTN_ROWS // rows)
    assert length % rows == 0 and rows % Q_BLOCK == 0 and rows >= 2 * Q_BLOCK and dilation % n_sub == 0
    n_tiles = length // rows
    halo_per_tile = rows // HALF
    n_halo = length // HALF
    width = dilation * GROUP_WIDTH
    step_width = n_sub * GROUP_WIDTH
    view = lambda t: t.reshape(batch, length, width)

    main = pl.BlockSpec((None, rows, step_width), lambda b, r, t: (b, t, r))
    left = pl.BlockSpec((None, HALF, step_width),
                        lambda b, r, t: (b, jnp.maximum(t * halo_per_tile - 1, 0), r))
    right = pl.BlockSpec((None, HALF, step_width),
                         lambda b, r, t: (b, jnp.minimum((t + 1) * halo_per_tile, n_halo - 1), r))
    bias = _attn_bias(group, dilation)
    tile_bytes = 2 * (3 * rows * step_width * 2 + 4 * HALF * step_width * 2
                      + rows * step_width * (2 + 4))
    out, lse = pl.pallas_call(
        _attn_kernel,
        grid=(batch, dilation // n_sub, n_tiles),
        in_specs=[main, main, left, right, main, left, right,
                  _const_spec(bias.shape)],
        out_specs=[main, main],
        out_shape=[jax.ShapeDtypeStruct((batch, length, width), BF16),
                   jax.ShapeDtypeStruct((batch, length, width), F32)],
        compiler_params=pltpu.CompilerParams(
            dimension_semantics=("parallel", "parallel", "parallel"),
            vmem_limit_bytes=_vmem_limit(tile_bytes + bias.size * 4 + (8 << 20))),
        name=f"attn_dil{dilation}",
    )(view(q), view(k), view(k), view(k), view(v), view(v), view(v), bias)
    return out.reshape(batch * length, width), lse.reshape(batch * length, width)


def _load_token_major(ref, dilation, slab_scr, stage_scr):
    if dilation == 1:
        return ref[...].astype(F32)
    n = ref.shape[0]
    step = DILATIONS[1]
    if dilation == step:
        for r in range(dilation):
            for s in range(N_SLABS):
                lo = r * GROUP_WIDTH + s * LANES
                slab_scr[s, pl.ds(r, n, stride=step), :] = ref[:, lo:lo + LANES].astype(F32)
    else:
        assert dilation == step * step
        rows = n * dilation
        for r in range(step):
            for s in range(N_SLABS):
                for rho in range(step):
                    lo = (r + step * rho) * GROUP_WIDTH + s * LANES
                    stage_scr[r, s, pl.ds(rho, n, stride=step), :] = ref[:, lo:lo + LANES].astype(F32)
                slab_scr[s, pl.ds(r, rows // step, stride=step), :] = stage_scr[r, s]
    return jnp.concatenate([slab_scr[s] for s in range(N_SLABS)], axis=1)


def _mix_out_units(x_ref, ma_ref, sgb_ref, o_refs, l_refs, wb_ref, wo_ref, gf_ref, wg_ref, wu_ref, wd_ref,
                   gl_ref, out_ref, slab_scr, stage_scr):
    lses, outs = [], []
    for i, (ref, d) in enumerate(zip(l_refs, DILATIONS)):
        lses.append(_load_token_major(ref, d, slab_scr.at[i], stage_scr.at[0]))
        yield
    for i, (ref, d) in enumerate(zip(o_refs, DILATIONS)):
        outs.append(_load_token_major(ref, d, slab_scr.at[N_GROUPS + i], stage_scr.at[1]))
        yield
    top = jnp.maximum(jnp.maximum(lses[0], lses[1]), lses[2])
    e0, e1, e2 = (jnp.exp2(l - top) for l in lses)
    y = (e0 * outs[0] + e1 * outs[1] + e2 * outs[2]) / (e0 + e1 + e2)
    yield
    merged = ma_ref[...] + sgb_ref[...] * _dot(y.astype(BF16), wb_ref[...])
    yield
    h = x_ref[...] + _dot(merged.astype(BF16), wo_ref[...])
    hn = _rms_norm(h, gf_ref[...]).astype(BF16)
    yield
    gate = jax.nn.silu(_dot(hn, wg_ref[...]))
    yield
    act = (gate * _dot(hn, wu_ref[...])).astype(BF16)
    yield
    h = h + _dot(act, wd_ref[...])
    yield
    out_ref[...] = _rms_norm(h, gl_ref[...])
    yield


MIX_OUT_CHAIN_UNITS = 2 * N_GROUPS + 3
MIX_OUT_FFN_MATMUL_UNITS = 3


def _mix_out_kernel(x_ref, ma_ref, sgb_ref, o0_ref, o1_ref, o2_ref, l0_ref, l1_ref, l2_ref,
                    wb_ref, wo_ref, gf_ref, wg_ref, wu_ref, wd_ref, gl_ref, out_ref, slab_scr, stage_scr):
    sub = x_ref.shape[0] // MIX_OUT_SUBTILES
    bodies = []
    for t in range(MIX_OUT_SUBTILES):
        rows_of = lambda ref, div=1: ref.at[pl.ds(t * sub // div, sub // div)]
        bodies.append(_mix_out_units(
            rows_of(x_ref), rows_of(ma_ref), rows_of(sgb_ref),
            [rows_of(ref, d) for ref, d in zip((o0_ref, o1_ref, o2_ref), DILATIONS)],
            [rows_of(ref, d) for ref, d in zip((l0_ref, l1_ref, l2_ref), DILATIONS)],
            wb_ref, wo_ref, gf_ref, wg_ref, wu_ref, wd_ref, gl_ref, rows_of(out_ref),
            slab_scr.at[t], stage_scr.at[t]))
    def advance(body, n):
        for _ in range(n):
            next(body)

    advance(bodies[0], MIX_OUT_CHAIN_UNITS)
    for t, body in enumerate(bodies):
        follower = bodies[t + 1] if t + 1 < len(bodies) else None
        for m in range(MIX_OUT_FFN_MATMUL_UNITS):
            advance(body, 1)
            if follower is not None:
                advance(follower, MIX_OUT_CHAIN_UNITS * (m + 1) // MIX_OUT_FFN_MATMUL_UNITS
                        - MIX_OUT_CHAIN_UNITS * m // MIX_OUT_FFN_MATMUL_UNITS)
        advance(body, 1)
        assert next(body, "end") == "end"


def _mix_out(x2d, ma, sgb, outs, lses, w_b, w_out, g_ffn, w_gate, w_up, w_down, g_final):
    tokens = x2d.shape[0]
    rows = MIX_OUT_ROWS
    sub = rows // MIX_OUT_SUBTILES
    assert tokens % rows == 0
    d_ff = w_gate.shape[1]
    row_spec = lambda width: pl.BlockSpec((rows, width), lambda i: (i, 0))
    assert all(sub % (BF16_SUBLANES * d) == 0 for d in DILATIONS)
    group_specs = [pl.BlockSpec((rows // d, d * GROUP_WIDTH), lambda i: (i, 0)) for d in DILATIONS]
    weights = (w_b, w_out, w_gate, w_up, w_down)
    weight_bytes = 2 * sum(w.size for w in weights)
    slab_shape = (MIX_OUT_SUBTILES, 2 * N_GROUPS, N_SLABS, sub, LANES)
    tile_bytes = 2 * rows * (4 * 4 * D_MODEL + 3 * GROUP_WIDTH * (2 + 4))
    step = DILATIONS[1]
    stage_shape = (MIX_OUT_SUBTILES, 2, step, N_SLABS, sub // step, LANES)
    temp_bytes = rows * (2 * 4 * d_ff + 2 * 4 * D_MODEL) + 4 * (math.prod(slab_shape) + math.prod(stage_shape))
    return pl.pallas_call(
        _mix_out_kernel,
        grid=(tokens // rows,),
        scratch_shapes=[pltpu.VMEM(slab_shape, F32), pltpu.VMEM(stage_shape, F32)],
        in_specs=[row_spec(D_MODEL)] * 3 + group_specs * 2 + [
            _const_spec(w_b.shape), _const_spec(w_out.shape), _const_spec((1, D_MODEL)),
            _const_spec(w_gate.shape), _const_spec(w_up.shape), _const_spec(w_down.shape),
            _const_spec((1, D_MODEL))],
        out_specs=row_spec(D_MODEL),
        out_shape=jax.ShapeDtypeStruct((tokens, D_MODEL), F32),
        compiler_params=pltpu.CompilerParams(
            dimension_semantics=("parallel",),
            vmem_limit_bytes=_vmem_limit(weight_bytes + tile_bytes + temp_bytes)),
        name="mix_out",
    )(x2d, ma, sgb, *outs, *lses, w_b, w_out, g_ffn, w_gate, w_up, w_down, g_final)


def kernel(x, norm_mix_g, w_in, gmlp_ln_g, gmlp_ln_b, gmlp_ws, gmlp_bs, w_branch_gmlp,
           w_branch_attn, w_out, norm_ffn_g, w_ffn_gate, w_ffn_up, w_ffn_down, norm_final_g):
    batch, seq, d_model = x.shape
    assert w_in.shape[0] == 1 and d_model == D_MODEL and w_in.shape[2] == IN_WIDTH
    for window, dilation in ATTN_PATTERNS:
        assert window == 2 * HALF * dilation
    tokens = batch * seq
    h = x.reshape(tokens, d_model)
    row = lambda v: v.reshape(1, -1).astype(F32)
    stage1 = _mix_in(
        h, row(norm_mix_g[0]), w_in[0].astype(BF16), row(gmlp_ln_g[0]), row(gmlp_ln_b[0]),
        gmlp_ws[0].astype(BF16), gmlp_bs[0].astype(F32).T, w_branch_gmlp[0].astype(BF16))
    qs, ks, vs = stage1[0:3], stage1[3:6], stage1[6:9]
    ma, sgb = stage1[9], stage1[10]
    outs, lses = [], []
    for gi, dilation in enumerate(DILATIONS):
        o, lse = _attention_group(gi, dilation, qs[gi], ks[gi], vs[gi], batch, seq)
        outs.append(o)
        lses.append(lse)
    out = _mix_out(h, ma, sgb, outs, lses, w_branch_attn[0].astype(BF16), w_out[0].astype(BF16),
                   row(norm_ffn_g[0]), w_ffn_gate[0].astype(BF16), w_ffn_up[0].astype(BF16),
                   w_ffn_down[0].astype(BF16), row(norm_final_g))
    return out.reshape(batch, seq, d_model)
```

```python
import functools
import math

import jax
import jax.numpy as jnp
import numpy as np
from jax import lax
from jax.experimental import pallas as pl
from jax.experimental.pallas import tpu as pltpu

F32 = jnp.float32
BF16 = jnp.bfloat16

D_MODEL = 1024
NORM_EPS = 1e-6
NEG_INF = -1e30
LOG2_E = math.log2(math.e)

CHUNK = 128
GMLP_WIDTH = D_MODEL
GMLP_GROUPS = 8

ATTN_PATTERNS = ((128, 1), (512, 4), (2048, 16))
DILATIONS = tuple(d for _, d in ATTN_PATTERNS)
N_GROUPS = len(ATTN_PATTERNS)
HEADS = 8
HEAD_DIM = 64
GROUP_WIDTH = HEADS * HEAD_DIM
QKV_WIDTH = N_GROUPS * GROUP_WIDTH
Q_BLOCK = 128
HALF = 64
KEY_WIN = Q_BLOCK + 2 * HALF
HEAD_PAIR = 2 * HEAD_DIM

OFF_U = 0
OFF_V = OFF_U + GMLP_WIDTH
OFF_Q = OFF_V + GMLP_WIDTH
OFF_K = OFF_Q + QKV_WIDTH
OFF_VA = OFF_K + QKV_WIDTH
OFF_GA = OFF_VA + QKV_WIDTH
OFF_GB = OFF_GA + D_MODEL
IN_WIDTH = OFF_GB + D_MODEL

LANES = 128
BF16_SUBLANES = 16
N_SLABS = GROUP_WIDTH // LANES
V7X_VMEM_BYTES = 64 * 1024 * 1024
MIX_IN_ROWS = 512
MIX_IN_SUBTILES = 2
MIX_OUT_ROWS = 512
MIX_OUT_SUBTILES = 2
ATTN_ROWS = 2048


def _vmem_limit(nbytes):
    return int(min(nbytes * 3 // 2, V7X_VMEM_BYTES - (4 << 20)))


def _const_spec(shape):
    zeros = (0,) * len(shape)
    return pl.BlockSpec(shape, lambda *_: zeros, pipeline_mode=pl.Buffered(1))


def _rms_norm(x, g):
    return x * lax.rsqrt(jnp.mean(x * x, axis=-1, keepdims=True) + NORM_EPS) * g


def _dot(a, b):
    return jnp.dot(a, b, preferred_element_type=F32)


def _gelu(x):
    return 0.5 * x * (1.0 + lax.erf(x * math.sqrt(0.5)))


def _subsequence_major_copies(xn, slab_scr, slab4_scr, perm_scr):
    rows = xn.shape[0]
    n_slabs = xn.shape[1] // LANES
    step = DILATIONS[1]
    assert DILATIONS == (1, step, step * step)
    for s in range(n_slabs):
        slab_scr[s] = xn[:, s * LANES:(s + 1) * LANES]
    n1 = rows // step
    for r in range(step):
        for s in range(n_slabs):
            piece = slab_scr[s, pl.ds(r, n1, stride=step), :]
            perm_scr[0, r * n1:(r + 1) * n1, s * LANES:(s + 1) * LANES] = piece.astype(BF16)
            slab4_scr[r, s] = piece
    n2 = n1 // step
    for r2 in range(step * step):
        r, rho = r2 % step, r2 // step
        for s in range(n_slabs):
            piece = slab4_scr[r, s, pl.ds(rho, n2, stride=step), :]
            perm_scr[1, r2 * n2:(r2 + 1) * n2, s * LANES:(s + 1) * LANES] = piece.astype(BF16)


def _store_lane_blocks(val, out_ref, dilation):
    n = val.shape[0] // dilation
    for r in range(dilation):
        out_ref[:, r * GROUP_WIDTH:(r + 1) * GROUP_WIDTH] = val[r * n:(r + 1) * n, :].astype(out_ref.dtype)


def _mix_in_units(gmlp_first, x_ref, g_ref, win_ref, lng_ref, lnb_ref, ws_ref, bst_ref, wa_ref,
                  qkv_refs, ma_ref, sgb_ref, y_scr, slab_scr, slab4_scr, perm_scr):
    rows = x_ref.shape[0]
    xn_f32 = _rms_norm(x_ref[...], g_ref[...])
    xn = xn_f32.astype(BF16)

    def proj(off, width, lhs=None):
        return _dot(xn if lhs is None else lhs, win_ref[:, off:off + width])

    def projection_units():
        _subsequence_major_copies(xn_f32, slab_scr, slab4_scr, perm_scr)
        yield
        scale = LOG2_E / math.sqrt(HEAD_DIM)
        for gi in range(N_GROUPS):
            off = gi * GROUP_WIDTH
            lhs = xn if gi == 0 else perm_scr[gi - 1]
            q_ref, k_ref, v_ref = qkv_refs[gi], qkv_refs[N_GROUPS + gi], qkv_refs[2 * N_GROUPS + gi]
            _store_lane_blocks(proj(OFF_Q + off, GROUP_WIDTH, lhs) * scale, q_ref, DILATIONS[gi])
            yield
            _store_lane_blocks(proj(OFF_K + off, GROUP_WIDTH, lhs), k_ref, DILATIONS[gi])
            yield
            _store_lane_blocks(proj(OFF_VA + off, GROUP_WIDTH, lhs), v_ref, DILATIONS[gi])
            yield
        sgb_ref[...] = jax.nn.sigmoid(proj(OFF_GB, D_MODEL))
        yield

    def gmlp_units():
        gu = _gelu(proj(OFF_U, GMLP_WIDTH))
        yield
        gv = _gelu(proj(OFF_V, GMLP_WIDTH))
        mean = jnp.mean(gv, axis=-1, keepdims=True)
        cen = gv - mean
        var = jnp.mean(cen * cen, axis=-1, keepdims=True)
        vn = (cen * lax.rsqrt(var + NORM_EPS) * lng_ref[...] + lnb_ref[...]).astype(BF16)
        yield
        for c in range(rows // CHUNK):
            rs = slice(c * CHUNK, (c + 1) * CHUNK)
            for g in range(GMLP_GROUPS):
                cs = slice(g * CHUNK, (g + 1) * CHUNK)
                mixed = _dot(ws_ref[g], vn[rs, cs]) + bst_ref[:, g:g + 1]
                y_scr[rs, cs] = (gu[rs, cs] * mixed).astype(BF16)
            yield
        gate = jax.nn.sigmoid(proj(OFF_GA, D_MODEL))
        yield
        ma_ref[...] = gate * _dot(y_scr[...], wa_ref[...])
        yield

    for units in ((gmlp_units, projection_units) if gmlp_first else (projection_units, gmlp_units)):
        yield from units()


def _mix_in_kernel(n_cast, x_ref, g_ref, win_ref, lng_ref, lnb_ref, ws_ref, bst_ref, wa_ref, *rest):
    cast_in, rest = rest[:n_cast], rest[n_cast:]
    qkv_refs, (ma_ref, sgb_ref, *rest) = rest[:3 * N_GROUPS], rest[3 * N_GROUPS:]
    cast_out, scratch = rest[:n_cast], rest[n_cast:]
    for src, dst in zip(cast_in, cast_out):
        dst[...] = src[...].astype(dst.dtype)
    sub = x_ref.shape[0] // MIX_IN_SUBTILES
    bodies = []
    for t in range(MIX_IN_SUBTILES):
        rows_of = lambda ref, div=1: ref.at[pl.ds(t * sub // div, sub // div)]
        bodies.append(_mix_in_units(
            t % 2 == 1, rows_of(x_ref), g_ref, win_ref, lng_ref, lnb_ref, ws_ref, bst_ref, wa_ref,
            [rows_of(ref, DILATIONS[i % N_GROUPS]) for i, ref in enumerate(qkv_refs)],
            rows_of(ma_ref), rows_of(sgb_ref), *(scr.at[t] for scr in scratch)))
    while bodies:
        bodies = [b for b in bodies if next(b, "end") != "end"]


def _cast_specs(weight, n_steps):
    n_rows, n_cols = weight.shape
    block = next(b for b in range(BF16_SUBLANES, n_rows + 1, BF16_SUBLANES)
                 if n_rows % b == 0 and n_rows // b <= n_steps)
    n_blocks = n_rows // block
    return pl.BlockSpec((block, n_cols), lambda i: (jnp.minimum(i, n_blocks - 1), 0))


def _mix_in(x2d, norm_g, w_in, ln_g, ln_b, ws, bs_t, w_a, later_weights):
    tokens = x2d.shape[0]
    rows = MIX_IN_ROWS
    sub = rows // MIX_IN_SUBTILES
    assert tokens % rows == 0 and sub % CHUNK == 0
    cast_specs = [_cast_specs(w, tokens // rows) for w in later_weights]
    row_spec = lambda width: pl.BlockSpec((rows, width), lambda i: (i, 0))
    assert all(sub % (BF16_SUBLANES * d) == 0 for d in DILATIONS)
    step = DILATIONS[1]
    qkv_specs = [pl.BlockSpec((rows // d, d * GROUP_WIDTH), lambda i: (i, 0)) for d in DILATIONS] * 3
    qkv_shapes = [jax.ShapeDtypeStruct((tokens // d, d * GROUP_WIDTH), BF16) for d in DILATIONS] * 3
    f32_shape = jax.ShapeDtypeStruct((tokens, D_MODEL), F32)
    weight_bytes = 2 * (w_in.size + w_a.size + ws.size)
    tile_bytes = 2 * rows * (4 * D_MODEL + 9 * 2 * GROUP_WIDTH + 2 * 4 * D_MODEL)
    temp_bytes = rows * D_MODEL * (2 + 2 * 4 + 2 * 2 + 3 * 4)
    return pl.pallas_call(
        functools.partial(_mix_in_kernel, len(later_weights)),
        grid=(tokens // rows,),
        in_specs=[
            row_spec(D_MODEL),
            _const_spec((1, D_MODEL)),
            _const_spec((D_MODEL, IN_WIDTH)),
            _const_spec((1, GMLP_WIDTH)),
            _const_spec((1, GMLP_WIDTH)),
            _const_spec((GMLP_GROUPS, CHUNK, CHUNK)),
            _const_spec((CHUNK, GMLP_GROUPS)),
            _const_spec((GMLP_WIDTH, D_MODEL)),
        ] + cast_specs,
        out_specs=qkv_specs + [row_spec(D_MODEL)] * 2 + cast_specs,
        out_shape=qkv_shapes + [f32_shape] * 2 + [jax.ShapeDtypeStruct(w.shape, BF16) for w in later_weights],
        scratch_shapes=[
            pltpu.VMEM((MIX_IN_SUBTILES, sub, GMLP_WIDTH), BF16),
            pltpu.VMEM((MIX_IN_SUBTILES, D_MODEL // LANES, sub, LANES), F32),
            pltpu.VMEM((MIX_IN_SUBTILES, step, D_MODEL // LANES, sub // step, LANES), F32),
            pltpu.VMEM((MIX_IN_SUBTILES, N_GROUPS - 1, sub, D_MODEL), BF16),
        ],
        compiler_params=pltpu.CompilerParams(
            dimension_semantics=("parallel",),
            vmem_limit_bytes=_vmem_limit(weight_bytes + tile_bytes + temp_bytes)),
        name="mix_in",
    )(x2d, norm_g, w_in, ln_g, ln_b, ws, bs_t, w_a, *later_weights)


def _attn_kernel(q_ref, k_ref, kl_ref, kr_ref, v_ref, vl_ref, vr_ref, bias_ref,
                 o_ref, lse_ref):
    rows = q_ref.shape[0]
    n_blk = rows // Q_BLOCK
    first = pl.program_id(2) == 0
    last = pl.program_id(2) == pl.num_programs(2) - 1
    lane = lax.broadcasted_iota(jnp.int32, (Q_BLOCK, HEAD_PAIR), 1)
    low_half = lane < HEAD_DIM
    ones_cols = jnp.ones((KEY_WIN, LANES), BF16)

    n_sub = q_ref.shape[1] // GROUP_WIDTH
    for j, u in ((j, u) for u in range(n_sub) for j in range(n_blk)):
        lo = j * Q_BLOCK - HALF
        if j == 0:
            edge = jnp.where(first, 1, 0)
        elif j == n_blk - 1:
            edge = jnp.where(last, 2, 0)
        else:
            edge = 0
        for p in range(HEADS // 2):
            cs = slice(u * GROUP_WIDTH + p * HEAD_PAIR, u * GROUP_WIDTH + (p + 1) * HEAD_PAIR)

            def window(main_ref, left_ref, right_ref):
                if j == 0:
                    return jnp.concatenate([left_ref[:, cs], main_ref[0:lo + KEY_WIN, cs]], axis=0)
                if j == n_blk - 1:
                    return jnp.concatenate([main_ref[lo:rows, cs], right_ref[:, cs]], axis=0)
                return main_ref[lo:lo + KEY_WIN, cs]

            k_win = window(k_ref, kl_ref, kr_ref)
            v_aug = jnp.concatenate([window(v_ref, vl_ref, vr_ref), ones_cols], axis=1)
            q_pair = q_ref[j * Q_BLOCK:(j + 1) * Q_BLOCK, cs]
            accs, maxes = [], []
            for e in range(2):
                own_lanes = low_half if e == 0 else jnp.logical_not(low_half)
                q_head = jnp.where(own_lanes, q_pair, jnp.zeros_like(q_pair))
                s = lax.dot_general(q_head, k_win, (((1,), (1,)), ((), ())),
                                    preferred_element_type=F32)
                s = s + bias_ref[edge, 2 * p + e]
                m = jnp.max(s, axis=-1, keepdims=True)
                prob = jnp.exp2(s - m).astype(BF16)
                accs.append(_dot(prob, v_aug))
                maxes.append(m)
            numer = jnp.where(low_half, accs[0][:, :LANES], accs[1][:, :LANES])
            denom = jnp.where(low_half, accs[0][:, LANES:], accs[1][:, LANES:])
            row_max = jnp.where(low_half, maxes[0], maxes[1])
            rs = slice(j * Q_BLOCK, (j + 1) * Q_BLOCK)
            o_ref[rs, cs] = (numer / denom).astype(o_ref.dtype)
            lse_ref[rs, cs] = row_max + jnp.log2(denom)


def _attn_bias(group, dilation):
    n = N_GROUPS * HEADS
    idx = np.arange(1, n + 1, dtype=np.float64)
    slopes = np.exp2(-8.0 * idx / n).reshape(N_GROUPS, HEADS)[group]
    qi = np.arange(Q_BLOCK)[:, None]
    kj = np.arange(KEY_WIN)[None, :]
    rel = np.abs(kj - HALF - qi)
    dist = (rel * dilation).astype(np.float64)
    band = rel <= HALF
    in_seq = np.stack([np.ones_like(kj, dtype=bool), kj >= HALF, kj < KEY_WIN - HALF])
    valid = band[None] & in_seq
    score_bias = -LOG2_E * slopes[:, None, None] * dist[None]
    return np.where(valid[:, None], score_bias[None], NEG_INF).astype(np.float32)


def _attention_group(group, dilation, q, k, v, batch, seq):
    length = seq // dilation
    rows = min(ATTN_ROWS, length)
    n_sub = min(dilation, ATTN_ROWS // rows)
    assert length % rows == 0 and rows % Q_BLOCK == 0 and rows >= 2 * Q_BLOCK and dilation % n_sub == 0
    n_tiles = length // rows
    halo_per_tile = rows // HALF
    n_halo = length // HALF
    width = dilation * GROUP_WIDTH
    step_width = n_sub * GROUP_WIDTH
    view = lambda t: t.reshape(batch, length, width)

    main = pl.BlockSpec((None, rows, step_width), lambda b, r, t: (b, t, r))
    left = pl.BlockSpec((None, HALF, step_width),
                        lambda b, r, t: (b, jnp.maximum(t * halo_per_tile - 1, 0), r))
    right = pl.BlockSpec((None, HALF, step_width),
                         lambda b, r, t: (b, jnp.minimum((t + 1) * halo_per_tile, n_halo - 1), r))
    bias = _attn_bias(group, dilation)
    tile_bytes = 2 * (3 * rows * step_width * 2 + 4 * HALF * step_width * 2
                      + rows * step_width * (2 + 4))
    out, lse = pl.pallas_call(
        _attn_kernel,
        grid=(batch, dilation // n_sub, n_tiles),
        in_specs=[main, main, left, right, main, left, right,
                  _const_spec(bias.shape)],
        out_specs=[main, main],
        out_shape=[jax.ShapeDtypeStruct((batch, length, width), BF16),
                   jax.ShapeDtypeStruct((batch, length, width), F32)],
        compiler_params=pltpu.CompilerParams(
            dimension_semantics=("parallel", "parallel", "parallel"),
            vmem_limit_bytes=_vmem_limit(tile_bytes + bias.size * 4 + (8 << 20))),
        name=f"attn_dil{dilation}",
    )(view(q), view(k), view(k), view(k), view(v), view(v), view(v), bias)
    return out.reshape(batch * length, width), lse.reshape(batch * length, width)


def _load_token_major(ref, dilation, slab_scr, stage_scr):
    if dilation == 1:
        return ref[...].astype(F32)
    n = ref.shape[0]
    step = DILATIONS[1]
    if dilation == step:
        for r in range(dilation):
            for s in range(N_SLABS):
                lo = r * GROUP_WIDTH + s * LANES
                slab_scr[s, pl.ds(r, n, stride=step), :] = ref[:, lo:lo + LANES].astype(F32)
    else:
        assert dilation == step * step
        rows = n * dilation
        for r in range(step):
            for s in range(N_SLABS):
                for rho in range(step):
                    lo = (r + step * rho) * GROUP_WIDTH + s * LANES
                    stage_scr[r, s, pl.ds(rho, n, stride=step), :] = ref[:, lo:lo + LANES].astype(F32)
                slab_scr[s, pl.ds(r, rows // step, stride=step), :] = stage_scr[r, s]
    return jnp.concatenate([slab_scr[s] for s in range(N_SLABS)], axis=1)


def _mix_out_units(x_ref, ma_ref, sgb_ref, o_refs, l_refs, wb_ref, wo_ref, gf_ref, wg_ref, wu_ref, wd_ref,
                   gl_ref, out_ref, slab_scr, stage_scr):
    lses, outs = [], []
    for i, (ref, d) in enumerate(zip(l_refs, DILATIONS)):
        lses.append(_load_token_major(ref, d, slab_scr.at[i], stage_scr.at[0]))
        yield
    for i, (ref, d) in enumerate(zip(o_refs, DILATIONS)):
        outs.append(_load_token_major(ref, d, slab_scr.at[N_GROUPS + i], stage_scr.at[1]))
        yield
    top = jnp.maximum(jnp.maximum(lses[0], lses[1]), lses[2])
    e0, e1, e2 = (jnp.exp2(l - top) for l in lses)
    y = (e0 * outs[0] + e1 * outs[1] + e2 * outs[2]) / (e0 + e1 + e2)
    yield
    merged = ma_ref[...] + sgb_ref[...] * _dot(y.astype(BF16), wb_ref[...])
    yield
    h = x_ref[...] + _dot(merged.astype(BF16), wo_ref[...])
    hn = _rms_norm(h, gf_ref[...]).astype(BF16)
    yield
    gate = jax.nn.silu(_dot(hn, wg_ref[...]))
    yield
    act = (gate * _dot(hn, wu_ref[...])).astype(BF16)
    yield
    h = h + _dot(act, wd_ref[...])
    yield
    out_ref[...] = _rms_norm(h, gl_ref[...])
    yield


MIX_OUT_CHAIN_UNITS = 2 * N_GROUPS + 3
MIX_OUT_FFN_MATMUL_UNITS = 3


def _mix_out_kernel(x_ref, ma_ref, sgb_ref, o0_ref, o1_ref, o2_ref, l0_ref, l1_ref, l2_ref,
                    wb_ref, wo_ref, gf_ref, wg_ref, wu_ref, wd_ref, gl_ref, out_ref, slab_scr, stage_scr):
    sub = x_ref.shape[0] // MIX_OUT_SUBTILES
    bodies = []
    for t in range(MIX_OUT_SUBTILES):
        rows_of = lambda ref, div=1: ref.at[pl.ds(t * sub // div, sub // div)]
        bodies.append(_mix_out_units(
            rows_of(x_ref), rows_of(ma_ref), rows_of(sgb_ref),
            [rows_of(ref, d) for ref, d in zip((o0_ref, o1_ref, o2_ref), DILATIONS)],
            [rows_of(ref, d) for ref, d in zip((l0_ref, l1_ref, l2_ref), DILATIONS)],
            wb_ref, wo_ref, gf_ref, wg_ref, wu_ref, wd_ref, gl_ref, rows_of(out_ref),
            slab_scr.at[t], stage_scr.at[t]))
    def advance(body, n):
        for _ in range(n):
            next(body)

    advance(bodies[0], MIX_OUT_CHAIN_UNITS)
    for t, body in enumerate(bodies):
        follower = bodies[t + 1] if t + 1 < len(bodies) else None
        spread = MIX_OUT_FFN_MATMUL_UNITS - 1
        for m in range(MIX_OUT_FFN_MATMUL_UNITS):
            advance(body, 1)
            if follower is not None:
                advance(follower, min(MIX_OUT_CHAIN_UNITS * (m + 1) // spread, MIX_OUT_CHAIN_UNITS)
                        - min(MIX_OUT_CHAIN_UNITS * m // spread, MIX_OUT_CHAIN_UNITS))
        advance(body, 1)
        assert next(body, "end") == "end"


def _mix_out(x2d, ma, sgb, outs, lses, w_b, w_out, g_ffn, w_gate, w_up, w_down, g_final):
    tokens = x2d.shape[0]
    rows = MIX_OUT_ROWS
    sub = rows // MIX_OUT_SUBTILES
    assert tokens % rows == 0
    d_ff = w_gate.shape[1]
    row_spec = lambda width: pl.BlockSpec((rows, width), lambda i: (i, 0))
    assert all(sub % (BF16_SUBLANES * d) == 0 for d in DILATIONS)
    group_specs = [pl.BlockSpec((rows // d, d * GROUP_WIDTH), lambda i: (i, 0)) for d in DILATIONS]
    weights = (w_b, w_out, w_gate, w_up, w_down)
    weight_bytes = 2 * sum(w.size for w in weights)
    slab_shape = (MIX_OUT_SUBTILES, 2 * N_GROUPS, N_SLABS, sub, LANES)
    tile_bytes = 2 * rows * (4 * 4 * D_MODEL + 3 * GROUP_WIDTH * (2 + 4))
    step = DILATIONS[1]
    stage_shape = (MIX_OUT_SUBTILES, 2, step, N_SLABS, sub // step, LANES)
    temp_bytes = rows * (2 * 4 * d_ff + 2 * 4 * D_MODEL) + 4 * (math.prod(slab_shape) + math.prod(stage_shape))
    return pl.pallas_call(
        _mix_out_kernel,
        grid=(tokens // rows,),
        scratch_shapes=[pltpu.VMEM(slab_shape, F32), pltpu.VMEM(stage_shape, F32)],
        in_specs=[row_spec(D_MODEL)] * 3 + group_specs * 2 + [
            _const_spec(w_b.shape), _const_spec(w_out.shape), _const_spec((1, D_MODEL)),
            _const_spec(w_gate.shape), _const_spec(w_up.shape), _const_spec(w_down.shape),
            _const_spec((1, D_MODEL))],
        out_specs=row_spec(D_MODEL),
        out_shape=jax.ShapeDtypeStruct((tokens, D_MODEL), F32),
        compiler_params=pltpu.CompilerParams(
            dimension_semantics=("parallel",),
            vmem_limit_bytes=_vmem_limit(weight_bytes + tile_bytes + temp_bytes)),
        name="mix_out",
    )(x2d, ma, sgb, *outs, *lses, w_b, w_out, g_ffn, w_gate, w_up, w_down, g_final)


def kernel(x, norm_mix_g, w_in, gmlp_ln_g, gmlp_ln_b, gmlp_ws, gmlp_bs, w_branch_gmlp,
           w_branch_attn, w_out, norm_ffn_g, w_ffn_gate, w_ffn_up, w_ffn_down, norm_final_g):
    batch, seq, d_model = x.shape
    assert w_in.shape[0] == 1 and d_model == D_MODEL and w_in.shape[2] == IN_WIDTH
    for window, dilation in ATTN_PATTERNS:
        assert window == 2 * HALF * dilation
    tokens = batch * seq
    h = x.reshape(tokens, d_model)
    row = lambda v: v.reshape(1, -1).astype(F32)
    later = [w[0].astype(F32) for w in (w_branch_attn, w_out, w_ffn_gate, w_ffn_up, w_ffn_down)]
    stage1 = _mix_in(
        h, row(norm_mix_g[0]), w_in[0].astype(BF16), row(gmlp_ln_g[0]), row(gmlp_ln_b[0]),
        gmlp_ws[0].astype(BF16), gmlp_bs[0].astype(F32).T, w_branch_gmlp[0].astype(BF16), later)
    qs, ks, vs = stage1[0:3], stage1[3:6], stage1[6:9]
    ma, sgb = stage1[9], stage1[10]
    w_b, w_o, w_gate, w_up, w_down = stage1[11:]
    outs, lses = [], []
    for gi, dilation in enumerate(DILATIONS):
        o, lse = _attention_group(gi, dilation, qs[gi], ks[gi], vs[gi], batch, seq)
        outs.append(o)
        lses.append(lse)
    out = _mix_out(h, ma, sgb, outs, lses, w_b, w_o, row(norm_ffn_g[0]), w_gate, w_up, w_down,
                   row(norm_final_g))
    return out.reshape(batch, seq, d_model)
```

```python
import functools
import math

import jax
import jax.numpy as jnp
import numpy as np
from jax import lax
from jax.experimental import pallas as pl
from jax.experimental.pallas import tpu as pltpu

F32 = jnp.float32
BF16 = jnp.bfloat16

D_MODEL = 1024
NORM_EPS = 1e-6
NEG_INF = -1e30
LOG2_E = math.log2(math.e)

CHUNK = 128
GMLP_WIDTH = D_MODEL
GMLP_GROUPS = 8

ATTN_PATTERNS = ((128, 1), (512, 4), (2048, 16))
DILATIONS = tuple(d for _, d in ATTN_PATTERNS)
N_GROUPS = len(ATTN_PATTERNS)
HEADS = 8
HEAD_DIM = 64
GROUP_WIDTH = HEADS * HEAD_DIM
QKV_WIDTH = N_GROUPS * GROUP_WIDTH
Q_BLOCK = 128
HALF = 64
KEY_WIN = Q_BLOCK + 2 * HALF
HEAD_PAIR = 2 * HEAD_DIM

OFF_U = 0
OFF_V = OFF_U + GMLP_WIDTH
OFF_Q = OFF_V + GMLP_WIDTH
OFF_K = OFF_Q + QKV_WIDTH
OFF_VA = OFF_K + QKV_WIDTH
OFF_GA = OFF_VA + QKV_WIDTH
OFF_GB = OFF_GA + D_MODEL
IN_WIDTH = OFF_GB + D_MODEL

LANES = 128
BF16_SUBLANES = 16
N_SLABS = GROUP_WIDTH // LANES
V7X_VMEM_BYTES = 64 * 1024 * 1024
MIX_IN_ROWS = 512
MIX_IN_SUBTILES = 2
MIX_OUT_ROWS = 512
MIX_OUT_SUBTILES = 2
ATTN_ROWS = 2048


def _vmem_limit(nbytes):
    return int(min(nbytes * 3 // 2, V7X_VMEM_BYTES - (4 << 20)))


def _const_spec(shape):
    zeros = (0,) * len(shape)
    return pl.BlockSpec(shape, lambda *_: zeros, pipeline_mode=pl.Buffered(1))


def _rms_norm(x, g):
    return x * lax.rsqrt(jnp.mean(x * x, axis=-1, keepdims=True) + NORM_EPS) * g


def _dot(a, b):
    return jnp.dot(a, b, preferred_element_type=F32)


def _gelu(x):
    return 0.5 * x * (1.0 + lax.erf(x * math.sqrt(0.5)))


def _subsequence_major_copies(xn, slab_scr, slab4_scr, perm_scr):
    rows = xn.shape[0]
    n_slabs = xn.shape[1] // LANES
    step = DILATIONS[1]
    assert DILATIONS == (1, step, step * step)
    for s in range(n_slabs):
        slab_scr[s] = xn[:, s * LANES:(s + 1) * LANES]
    n1 = rows // step
    for r in range(step):
        for s in range(n_slabs):
            piece = slab_scr[s, pl.ds(r, n1, stride=step), :]
            perm_scr[0, r * n1:(r + 1) * n1, s * LANES:(s + 1) * LANES] = piece.astype(BF16)
            slab4_scr[r, s] = piece
    n2 = n1 // step
    for r2 in range(step * step):
        r, rho = r2 % step, r2 // step
        for s in range(n_slabs):
            piece = slab4_scr[r, s, pl.ds(rho, n2, stride=step), :]
            perm_scr[1, r2 * n2:(r2 + 1) * n2, s * LANES:(s + 1) * LANES] = piece.astype(BF16)


def _store_lane_blocks(val, out_ref, dilation):
    n = val.shape[0] // dilation
    for r in range(dilation):
        out_ref[:, r * GROUP_WIDTH:(r + 1) * GROUP_WIDTH] = val[r * n:(r + 1) * n, :].astype(out_ref.dtype)


def _mix_in_units(gmlp_first, x_ref, g_ref, win_ref, lng_ref, lnb_ref, ws_ref, bst_ref, wa_ref,
                  qkv_refs, ma_ref, sgb_ref, y_scr, slab_scr, slab4_scr, perm_scr):
    rows = x_ref.shape[0]
    xn_f32 = _rms_norm(x_ref[...], g_ref[...])
    xn = xn_f32.astype(BF16)

    def proj(off, width, lhs=None):
        return _dot(xn if lhs is None else lhs, win_ref[:, off:off + width])

    def projection_units():
        scale = LOG2_E / math.sqrt(HEAD_DIM)
        for gi in range(N_GROUPS):
            if gi == 1:
                _subsequence_major_copies(xn_f32, slab_scr, slab4_scr, perm_scr)
                yield
            off = gi * GROUP_WIDTH
            lhs = xn if gi == 0 else perm_scr[gi - 1]
            q_ref, k_ref, v_ref = qkv_refs[gi], qkv_refs[N_GROUPS + gi], qkv_refs[2 * N_GROUPS + gi]
            _store_lane_blocks(proj(OFF_Q + off, GROUP_WIDTH, lhs) * scale, q_ref, DILATIONS[gi])
            yield
            _store_lane_blocks(proj(OFF_K + off, GROUP_WIDTH, lhs), k_ref, DILATIONS[gi])
            yield
            _store_lane_blocks(proj(OFF_VA + off, GROUP_WIDTH, lhs), v_ref, DILATIONS[gi])
            yield
        sgb_ref[...] = jax.nn.sigmoid(proj(OFF_GB, D_MODEL))
        yield

    def gmlp_units():
        gu = _gelu(proj(OFF_U, GMLP_WIDTH))
        yield
        gv = _gelu(proj(OFF_V, GMLP_WIDTH))
        mean = jnp.mean(gv, axis=-1, keepdims=True)
        cen = gv - mean
        var = jnp.mean(cen * cen, axis=-1, keepdims=True)
        vn = (cen * lax.rsqrt(var + NORM_EPS) * lng_ref[...] + lnb_ref[...]).astype(BF16)
        yield
        for c in range(rows // CHUNK):
            rs = slice(c * CHUNK, (c + 1) * CHUNK)
            for g in range(GMLP_GROUPS):
                cs = slice(g * CHUNK, (g + 1) * CHUNK)
                mixed = _dot(ws_ref[g], vn[rs, cs]) + bst_ref[:, g:g + 1]
                y_scr[rs, cs] = (gu[rs, cs] * mixed).astype(BF16)
            yield
        gate = jax.nn.sigmoid(proj(OFF_GA, D_MODEL))
        yield
        ma_ref[...] = gate * _dot(y_scr[...], wa_ref[...])
        yield

    for units in ((gmlp_units, projection_units) if gmlp_first else (projection_units, gmlp_units)):
        yield from units()


def _mix_in_kernel(n_cast, x_ref, g_ref, win_ref, lng_ref, lnb_ref, ws_ref, bst_ref, wa_ref, *rest):
    cast_in, rest = rest[:n_cast], rest[n_cast:]
    qkv_refs, (ma_ref, sgb_ref, *rest) = rest[:3 * N_GROUPS], rest[3 * N_GROUPS:]
    cast_out, scratch = rest[:n_cast], rest[n_cast:]
    for src, dst in zip(cast_in, cast_out):
        dst[...] = src[...].astype(dst.dtype)
    sub = x_ref.shape[0] // MIX_IN_SUBTILES
    bodies = []
    for t in range(MIX_IN_SUBTILES):
        rows_of = lambda ref, div=1: ref.at[pl.ds(t * sub // div, sub // div)]
        bodies.append(_mix_in_units(
            t % 2 == 1, rows_of(x_ref), g_ref, win_ref, lng_ref, lnb_ref, ws_ref, bst_ref, wa_ref,
            [rows_of(ref, DILATIONS[i % N_GROUPS]) for i, ref in enumerate(qkv_refs)],
            rows_of(ma_ref), rows_of(sgb_ref), *(scr.at[t] for scr in scratch)))
    while bodies:
        bodies = [b for b in bodies if next(b, "end") != "end"]


def _cast_specs(weight, n_steps):
    n_rows, n_cols = weight.shape
    block = next(b for b in range(BF16_SUBLANES, n_rows + 1, BF16_SUBLANES)
                 if n_rows % b == 0 and n_rows // b <= n_steps)
    n_blocks = n_rows // block
    return pl.BlockSpec((block, n_cols), lambda i: (jnp.minimum(i, n_blocks - 1), 0))


def _mix_in(x2d, norm_g, w_in, ln_g, ln_b, ws, bs_t, w_a, later_weights):
    tokens = x2d.shape[0]
    rows = MIX_IN_ROWS
    sub = rows // MIX_IN_SUBTILES
    assert tokens % rows == 0 and sub % CHUNK == 0
    cast_specs = [_cast_specs(w, tokens // rows) for w in later_weights]
    row_spec = lambda width: pl.BlockSpec((rows, width), lambda i: (i, 0))
    assert all(sub % (BF16_SUBLANES * d) == 0 for d in DILATIONS)
    step = DILATIONS[1]
    qkv_specs = [pl.BlockSpec((rows // d, d * GROUP_WIDTH), lambda i: (i, 0)) for d in DILATIONS] * 3
    qkv_shapes = [jax.ShapeDtypeStruct((tokens // d, d * GROUP_WIDTH), BF16) for d in DILATIONS] * 3
    f32_shape = jax.ShapeDtypeStruct((tokens, D_MODEL), F32)
    weight_bytes = 2 * (w_in.size + w_a.size + ws.size)
    tile_bytes = 2 * rows * (4 * D_MODEL + 9 * 2 * GROUP_WIDTH + 2 * 4 * D_MODEL)
    temp_bytes = rows * D_MODEL * (2 + 2 * 4 + 2 * 2 + 3 * 4)
    return pl.pallas_call(
        functools.partial(_mix_in_kernel, len(later_weights)),
        grid=(tokens // rows,),
        in_specs=[
            row_spec(D_MODEL),
            _const_spec((1, D_MODEL)),
            _const_spec((D_MODEL, IN_WIDTH)),
            _const_spec((1, GMLP_WIDTH)),
            _const_spec((1, GMLP_WIDTH)),
            _const_spec((GMLP_GROUPS, CHUNK, CHUNK)),
            _const_spec((CHUNK, GMLP_GROUPS)),
            _const_spec((GMLP_WIDTH, D_MODEL)),
        ] + cast_specs,
        out_specs=qkv_specs + [row_spec(D_MODEL)] * 2 + cast_specs,
        out_shape=qkv_shapes + [f32_shape] * 2 + [jax.ShapeDtypeStruct(w.shape, BF16) for w in later_weights],
        scratch_shapes=[
            pltpu.VMEM((MIX_IN_SUBTILES, sub, GMLP_WIDTH), BF16),
            pltpu.VMEM((MIX_IN_SUBTILES, D_MODEL // LANES, sub, LANES), F32),
            pltpu.VMEM((MIX_IN_SUBTILES, step, D_MODEL // LANES, sub // step, LANES), F32),
            pltpu.VMEM((MIX_IN_SUBTILES, N_GROUPS - 1, sub, D_MODEL), BF16),
        ],
        compiler_params=pltpu.CompilerParams(
            dimension_semantics=("parallel",),
            vmem_limit_bytes=_vmem_limit(weight_bytes + tile_bytes + temp_bytes)),
        name="mix_in",
    )(x2d, norm_g, w_in, ln_g, ln_b, ws, bs_t, w_a, *later_weights)


def _attn_kernel(q_ref, k_ref, kl_ref, kr_ref, v_ref, vl_ref, vr_ref, bias_ref,
                 o_ref, lse_ref):
    rows = q_ref.shape[0]
    n_blk = rows // Q_BLOCK
    first = pl.program_id(2) == 0
    last = pl.program_id(2) == pl.num_programs(2) - 1
    lane = lax.broadcasted_iota(jnp.int32, (Q_BLOCK, HEAD_PAIR), 1)
    low_half = lane < HEAD_DIM
    ones_cols = jnp.ones((KEY_WIN, LANES), BF16)

    n_sub = q_ref.shape[1] // GROUP_WIDTH
    for j, u in ((j, u) for u in range(n_sub) for j in range(n_blk)):
        lo = j * Q_BLOCK - HALF
        if j == 0:
            edge = jnp.where(first, 1, 0)
        elif j == n_blk - 1:
            edge = jnp.where(last, 2, 0)
        else:
            edge = 0
        for p in range(HEADS // 2):
            cs = slice(u * GROUP_WIDTH + p * HEAD_PAIR, u * GROUP_WIDTH + (p + 1) * HEAD_PAIR)

            def window(main_ref, left_ref, right_ref):
                if j == 0:
                    return jnp.concatenate([left_ref[:, cs], main_ref[0:lo + KEY_WIN, cs]], axis=0)
                if j == n_blk - 1:
                    return jnp.concatenate([main_ref[lo:rows, cs], right_ref[:, cs]], axis=0)
                return main_ref[lo:lo + KEY_WIN, cs]

            k_win = window(k_ref, kl_ref, kr_ref)
            v_aug = jnp.concatenate([window(v_ref, vl_ref, vr_ref), ones_cols], axis=1)
            q_pair = q_ref[j * Q_BLOCK:(j + 1) * Q_BLOCK, cs]
            accs, maxes = [], []
            for e in range(2):
                own_lanes = low_half if e == 0 else jnp.logical_not(low_half)
                q_head = jnp.where(own_lanes, q_pair, jnp.zeros_like(q_pair))
                s = lax.dot_general(q_head, k_win, (((1,), (1,)), ((), ())),
                                    preferred_element_type=F32)
                s = s + bias_ref[edge, 2 * p + e]
                m = jnp.max(s, axis=-1, keepdims=True)
                prob = jnp.exp2(s - m).astype(BF16)
                accs.append(_dot(prob, v_aug))
                maxes.append(m)
            numer = jnp.where(low_half, accs[0][:, :LANES], accs[1][:, :LANES])
            denom = jnp.where(low_half, accs[0][:, LANES:], accs[1][:, LANES:])
            row_max = jnp.where(low_half, maxes[0], maxes[1])
            rs = slice(j * Q_BLOCK, (j + 1) * Q_BLOCK)
            o_ref[rs, cs] = (numer / denom).astype(o_ref.dtype)
            lse_ref[rs, cs] = row_max + jnp.log2(denom)


def _attn_bias(group, dilation):
    n = N_GROUPS * HEADS
    idx = np.arange(1, n + 1, dtype=np.float64)
    slopes = np.exp2(-8.0 * idx / n).reshape(N_GROUPS, HEADS)[group]
    qi = np.arange(Q_BLOCK)[:, None]
    kj = np.arange(KEY_WIN)[None, :]
    rel = np.abs(kj - HALF - qi)
    dist = (rel * dilation).astype(np.float64)
    band = rel <= HALF
    in_seq = np.stack([np.ones_like(kj, dtype=bool), kj >= HALF, kj < KEY_WIN - HALF])
    valid = band[None] & in_seq
    score_bias = -LOG2_E * slopes[:, None, None] * dist[None]
    return np.where(valid[:, None], score_bias[None], NEG_INF).astype(np.float32)


def _attention_group(group, dilation, q, k, v, batch, seq):
    length = seq // dilation
    rows = min(ATTN_ROWS, length)
    n_sub = min(dilation, ATTN_ROWS // rows)
    assert length % rows == 0 and rows % Q_BLOCK == 0 and rows >= 2 * Q_BLOCK and dilation % n_sub == 0
    n_tiles = length // rows
    halo_per_tile = rows // HALF
    n_halo = length // HALF
    width = dilation * GROUP_WIDTH
    step_width = n_sub * GROUP_WIDTH
    view = lambda t: t.reshape(batch, length, width)

    main = pl.BlockSpec((None, rows, step_width), lambda b, r, t: (b, t, r))
    left = pl.BlockSpec((None, HALF, step_width),
                        lambda b, r, t: (b, jnp.maximum(t * halo_per_tile - 1, 0), r))
    right = pl.BlockSpec((None, HALF, step_width),
                         lambda b, r, t: (b, jnp.minimum((t + 1) * halo_per_tile, n_halo - 1), r))
    bias = _attn_bias(group, dilation)
    tile_bytes = 2 * (3 * rows * step_width * 2 + 4 * HALF * step_width * 2
                      + rows * step_width * (2 + 4))
    out, lse = pl.pallas_call(
        _attn_kernel,
        grid=(batch, dilation // n_sub, n_tiles),
        in_specs=[main, main, left, right, main, left, right,
                  _const_spec(bias.shape)],
        out_specs=[main, main],
        out_shape=[jax.ShapeDtypeStruct((batch, length, width), BF16),
                   jax.ShapeDtypeStruct((batch, length, width), F32)],
        compiler_params=pltpu.CompilerParams(
            dimension_semantics=("parallel", "parallel", "parallel"),
            vmem_limit_bytes=_vmem_limit(tile_bytes + bias.size * 4 + (8 << 20))),
        name=f"attn_dil{dilation}",
    )(view(q), view(k), view(k), view(k), view(v), view(v), view(v), bias)
    return out.reshape(batch * length, width), lse.reshape(batch * length, width)


def _load_token_major(ref, dilation, slab_scr, stage_scr):
    if dilation == 1:
        return ref[...].astype(F32)
    n = ref.shape[0]
    step = DILATIONS[1]
    if dilation == step:
        for r in range(dilation):
            for s in range(N_SLABS):
                lo = r * GROUP_WIDTH + s * LANES
                slab_scr[s, pl.ds(r, n, stride=step), :] = ref[:, lo:lo + LANES].astype(F32)
    else:
        assert dilation == step * step
        rows = n * dilation
        for r in range(step):
            for s in range(N_SLABS):
                for rho in range(step):
                    lo = (r + step * rho) * GROUP_WIDTH + s * LANES
                    stage_scr[r, s, pl.ds(rho, n, stride=step), :] = ref[:, lo:lo + LANES].astype(F32)
                slab_scr[s, pl.ds(r, rows // step, stride=step), :] = stage_scr[r, s]
    return jnp.concatenate([slab_scr[s] for s in range(N_SLABS)], axis=1)


def _mix_out_units(x_ref, ma_ref, sgb_ref, o_refs, l_refs, wb_ref, wo_ref, gf_ref, wg_ref, wu_ref, wd_ref,
                   gl_ref, out_ref, slab_scr, stage_scr):
    lses, outs = [], []
    for i, (ref, d) in enumerate(zip(l_refs, DILATIONS)):
        lses.append(_load_token_major(ref, d, slab_scr.at[i], stage_scr.at[0]))
        yield
    for i, (ref, d) in enumerate(zip(o_refs, DILATIONS)):
        outs.append(_load_token_major(ref, d, slab_scr.at[N_GROUPS + i], stage_scr.at[1]))
        yield
    top = jnp.maximum(jnp.maximum(lses[0], lses[1]), lses[2])
    e0, e1, e2 = (jnp.exp2(l - top) for l in lses)
    y = (e0 * outs[0] + e1 * outs[1] + e2 * outs[2]) / (e0 + e1 + e2)
    yield
    merged = ma_ref[...] + sgb_ref[...] * _dot(y.astype(BF16), wb_ref[...])
    yield
    h = x_ref[...] + _dot(merged.astype(BF16), wo_ref[...])
    hn = _rms_norm(h, gf_ref[...]).astype(BF16)
    yield
    gate = jax.nn.silu(_dot(hn, wg_ref[...]))
    yield
    act = (gate * _dot(hn, wu_ref[...])).astype(BF16)
    yield
    h = h + _dot(act, wd_ref[...])
    yield
    out_ref[...] = _rms_norm(h, gl_ref[...])
    yield


MIX_OUT_CHAIN_UNITS = 2 * N_GROUPS + 3
MIX_OUT_FFN_MATMUL_UNITS = 3


def _mix_out_kernel(x_ref, ma_ref, sgb_ref, o0_ref, o1_ref, o2_ref, l0_ref, l1_ref, l2_ref,
                    wb_ref, wo_ref, gf_ref, wg_ref, wu_ref, wd_ref, gl_ref, out_ref, slab_scr, stage_scr):
    sub = x_ref.shape[0] // MIX_OUT_SUBTILES
    bodies = []
    for t in range(MIX_OUT_SUBTILES):
        rows_of = lambda ref, div=1: ref.at[pl.ds(t * sub // div, sub // div)]
        bodies.append(_mix_out_units(
            rows_of(x_ref), rows_of(ma_ref), rows_of(sgb_ref),
            [rows_of(ref, d) for ref, d in zip((o0_ref, o1_ref, o2_ref), DILATIONS)],
            [rows_of(ref, d) for ref, d in zip((l0_ref, l1_ref, l2_ref), DILATIONS)],
            wb_ref, wo_ref, gf_ref, wg_ref, wu_ref, wd_ref, gl_ref, rows_of(out_ref),
            slab_scr.at[t], stage_scr.at[t]))
    def advance(body, n):
        for _ in range(n):
            next(body)

    advance(bodies[0], MIX_OUT_CHAIN_UNITS)
    for t, body in enumerate(bodies):
        follower = bodies[t + 1] if t + 1 < len(bodies) else None
        spread = MIX_OUT_FFN_MATMUL_UNITS - 1
        for m in range(MIX_OUT_FFN_MATMUL_UNITS):
            advance(body, 1)
            if follower is not None:
                advance(follower, min(MIX_OUT_CHAIN_UNITS * (m + 1) // spread, MIX_OUT_CHAIN_UNITS)
                        - min(MIX_OUT_CHAIN_UNITS * m // spread, MIX_OUT_CHAIN_UNITS))
        advance(body, 1)
        assert next(body, "end") == "end"


def _mix_out(x2d, ma, sgb, outs, lses, w_b, w_out, g_ffn, w_gate, w_up, w_down, g_final):
    tokens = x2d.shape[0]
    rows = MIX_OUT_ROWS
    sub = rows // MIX_OUT_SUBTILES
    assert tokens % rows == 0
    d_ff = w_gate.shape[1]
    row_spec = lambda width: pl.BlockSpec((rows, width), lambda i: (i, 0))
    assert all(sub % (BF16_SUBLANES * d) == 0 for d in DILATIONS)
    group_specs = [pl.BlockSpec((rows // d, d * GROUP_WIDTH), lambda i: (i, 0)) for d in DILATIONS]
    weights = (w_b, w_out, w_gate, w_up, w_down)
    weight_bytes = 2 * sum(w.size for w in weights)
    slab_shape = (MIX_OUT_SUBTILES, 2 * N_GROUPS, N_SLABS, sub, LANES)
    tile_bytes = 2 * rows * (4 * 4 * D_MODEL + 3 * GROUP_WIDTH * (2 + 4))
    step = DILATIONS[1]
    stage_shape = (MIX_OUT_SUBTILES, 2, step, N_SLABS, sub // step, LANES)
    temp_bytes = rows * (2 * 4 * d_ff + 2 * 4 * D_MODEL) + 4 * (math.prod(slab_shape) + math.prod(stage_shape))
    return pl.pallas_call(
        _mix_out_kernel,
        grid=(tokens // rows,),
        scratch_shapes=[pltpu.VMEM(slab_shape, F32), pltpu.VMEM(stage_shape, F32)],
        in_specs=[row_spec(D_MODEL)] * 3 + group_specs * 2 + [
            _const_spec(w_b.shape), _const_spec(w_out.shape), _const_spec((1, D_MODEL)),
            _const_spec(w_gate.shape), _const_spec(w_up.shape), _const_spec(w_down.shape),
            _const_spec((1, D_MODEL))],
        out_specs=row_spec(D_MODEL),
        out_shape=jax.ShapeDtypeStruct((tokens, D_MODEL), F32),
        compiler_params=pltpu.CompilerParams(
            dimension_semantics=("parallel",),
            vmem_limit_bytes=_vmem_limit(weight_bytes + tile_bytes + temp_bytes)),
        name="mix_out",
    )(x2d, ma, sgb, *outs, *lses, w_b, w_out, g_ffn, w_gate, w_up, w_down, g_final)


def kernel(x, norm_mix_g, w_in, gmlp_ln_g, gmlp_ln_b, gmlp_ws, gmlp_bs, w_branch_gmlp,
           w_branch_attn, w_out, norm_ffn_g, w_ffn_gate, w_ffn_up, w_ffn_down, norm_final_g):
    batch, seq, d_model = x.shape
    assert w_in.shape[0] == 1 and d_model == D_MODEL and w_in.shape[2] == IN_WIDTH
    for window, dilation in ATTN_PATTERNS:
        assert window == 2 * HALF * dilation
    tokens = batch * seq
    h = x.reshape(tokens, d_model)
    row = lambda v: v.reshape(1, -1).astype(F32)
    later = [w[0].astype(F32) for w in (w_branch_attn, w_out, w_ffn_gate, w_ffn_up, w_ffn_down)]
    stage1 = _mix_in(
        h, row(norm_mix_g[0]), w_in[0].astype(BF16), row(gmlp_ln_g[0]), row(gmlp_ln_b[0]),
        gmlp_ws[0].astype(BF16), gmlp_bs[0].astype(F32).T, w_branch_gmlp[0].astype(BF16), later)
    qs, ks, vs = stage1[0:3], stage1[3:6], stage1[6:9]
    ma, sgb = stage1[9], stage1[10]
    w_b, w_o, w_gate, w_up, w_down = stage1[11:]
    outs, lses = [], []
    for gi, dilation in enumerate(DILATIONS):
        o, lse = _attention_group(gi, dilation, qs[gi], ks[gi], vs[gi], batch, seq)
        outs.append(o)
        lses.append(lse)
    out = _mix_out(h, ma, sgb, outs, lses, w_b, w_o, row(norm_ffn_g[0]), w_gate, w_up, w_down,
                   row(norm_final_g))
    return out.reshape(batch, seq, d_model)
```
